```python
import math
import jax, jax.numpy as jnp
from jax import lax
import numpy as np

D_MODEL = 1024
BATCH = 2
SEQ = 16384
DEPTH = 1
DEC_BATCH = 128
DEC_SEQ = 8
PAST_LEN = 8192
PAGE_SIZE = 128

N_META = 16
SSM_WIDTH = D_MODEL // 2
SSM_GROUP = 16
SSM_GROUPS = SSM_WIDTH // SSM_GROUP
SSM_STATE = 64
ATTN_WIDTH = D_MODEL - SSM_WIDTH
ATTN_DV = 128
N_HEADS = ATTN_WIDTH // ATTN_DV
ATTN_DK = ATTN_DV // 2
IN_WIDTH = SSM_WIDTH + 3 * ATTN_WIDTH
ROPE_THETA = 10000.0
Q_BLOCK = 128
N_KEYS = 128
N_EXPERTS = N_KEYS * N_KEYS
PEER_HEADS = 8
PEER_TOPK = 16
PEER_DQ = D_MODEL // 4
PEER_BLOCK = 256
EPS = 1e-6
NEG_INF = -1e30
DT_MIN = 0.001
DT_MAX = 0.1

kernel_name = "hymba_s5_diffattn_peer_step"


def lambda_init(layer):
    return 0.8 - 0.6 * math.exp(-0.3 * layer)


def rmsnorm(x, g):
    xf = x.astype(jnp.float32)
    return xf * lax.rsqrt(jnp.mean(xf * xf, axis=-1, keepdims=True) + EPS) * g.astype(jnp.float32)


def rope(x, pos):
    half = ATTN_DK // 2
    inv_freq = ROPE_THETA ** (-jnp.arange(half, dtype=jnp.float32) * 2.0 / ATTN_DK)
    ang = pos[:, None] * inv_freq[None, :]
    cos = jnp.cos(ang)[:, None, None, :]
    sin = jnp.sin(ang)[:, None, None, :]
    x1, x2 = x[..., :half], x[..., half:]
    return jnp.concatenate([x1 * cos - x2 * sin, x2 * cos + x1 * sin], axis=-1)


def project(h, pos, norm_g, w_in, q_g, k_g):
    n = rmsnorm(h, norm_g)
    z = n @ w_in.astype(jnp.float32)
    lead = z.shape[:-1]
    u = z[..., :SSM_WIDTH]
    q = z[..., SSM_WIDTH:SSM_WIDTH + ATTN_WIDTH].reshape(*lead, N_HEADS, 2, ATTN_DK)
    k = z[..., SSM_WIDTH + ATTN_WIDTH:SSM_WIDTH + 2 * ATTN_WIDTH].reshape(*lead, N_HEADS, 2, ATTN_DK)
    v = z[..., SSM_WIDTH + 2 * ATTN_WIDTH:].reshape(*lead, N_HEADS, ATTN_DV)
    q = rope(rmsnorm(q, q_g), pos)
    k = rope(rmsnorm(k, k_g), pos)
    return u, q, k, v


def _ssm_combine(e1, e2):
    a1r, a1i, b1r, b1i = e1
    a2r, a2i, b2r, b2i = e2
    return (a2r * a1r - a2i * a1i,
            a2r * a1i + a2i * a1r,
            a2r * b1r - a2i * b1i + b2r,
            a2r * b1i + a2i * b1r + b2i)


def s5_mixer(u, h0_re, h0_im, a_re, a_im, log_dt, b_re, b_im, c_re, c_im, d_skip, w_glu, b_glu):
    f32 = jnp.float32
    bt, seq_len, _ = u.shape
    ug = u.reshape(bt, seq_len, SSM_GROUPS, SSM_GROUP)
    a_re = a_re.astype(f32)
    a_im = a_im.astype(f32)
    dt = jnp.exp(log_dt.astype(f32))[:, None]
    mag = jnp.exp(dt * a_re)
    ab_re = mag * jnp.cos(dt * a_im)
    ab_im = mag * jnp.sin(dt * a_im)
    den = a_re * a_re + a_im * a_im
    z_re = ((ab_re - 1.0) * a_re + ab_im * a_im) / den
    z_im = (ab_im * a_re - (ab_re - 1.0) * a_im) / den
    b_re = b_re.astype(f32)
    b_im = b_im.astype(f32)
    bb_re = z_re[..., None] * b_re - z_im[..., None] * b_im
    bb_im = z_re[..., None] * b_im + z_im[..., None] * b_re
    bu_re = jnp.einsum('blgc,gnc->lbgn', ug, bb_re)
    bu_im = jnp.einsum('blgc,gnc->lbgn', ug, bb_im)
    ar = jnp.broadcast_to(ab_re[None, None], (seq_len, 1, SSM_GROUPS, SSM_STATE))
    ai = jnp.broadcast_to(ab_im[None, None], (seq_len, 1, SSM_GROUPS, SSM_STATE))
    acum_re, acum_im, x_re, x_im = lax.associative_scan(_ssm_combine, (ar, ai, bu_re, bu_im), axis=0)
    h0_re = h0_re.astype(f32)
    h0_im = h0_im.astype(f32)
    x_re, x_im = (x_re + acum_re * h0_re - acum_im * h0_im,
                  x_im + acum_re * h0_im + acum_im * h0_re)
    y = jnp.einsum('lbgn,gcn->blgc', x_re, c_re.astype(f32)) - jnp.einsum('lbgn,gcn->blgc', x_im, c_im.astype(f32))
    y = y.reshape(bt, seq_len, SSM_WIDTH) + d_skip.astype(f32) * u
    y = jax.nn.gelu(y)
    y = y * jax.nn.sigmoid(y @ w_glu.astype(f32) + b_glu.astype(f32))
    return y, x_re[-1], x_im[-1]


def diff_weights(s, lam):
    p = jax.nn.softmax(s, axis=-1)
    return p[..., 0, :, :] - lam * p[..., 1, :, :]


def attn_prompt(q, k, v, lam):
    bt, seq_len = q.shape[:2]
    nb = -(-seq_len // Q_BLOCK)
    lpad = nb * Q_BLOCK
    qp = jnp.pad(q, ((0, 0), (0, lpad - seq_len), (0, 0), (0, 0), (0, 0)))
    qb = qp.reshape(bt, nb, Q_BLOCK, N_HEADS, 2, ATTN_DK).transpose(1, 0, 2, 3, 4, 5)
    kpos = jnp.arange(seq_len)
    scale = ATTN_DK ** -0.5

    def block(args):
        qblk, i = args
        qpos = i * Q_BLOCK + jnp.arange(Q_BLOCK)
        s = jnp.einsum('bqhcd,bkhcd->bhcqk', qblk, k) * scale
        s = jnp.where(kpos[None, :] <= qpos[:, None], s, NEG_INF)
        w = diff_weights(s, lam)
        return jnp.einsum('bhqk,bkhd->bqhd', w, v)

    o = lax.map(block, (qb, jnp.arange(nb)))
    return o.transpose(1, 0, 2, 3, 4).reshape(bt, lpad, N_HEADS, ATTN_DV)[:, :seq_len]


def attn_sample(q, k, v, lam, cache_k, cache_v, page_table, layer):
    past = page_table.shape[1] * cache_k.shape[2]
    s_len = q.shape[1]
    kidx = jnp.arange(past + s_len)
    qidx = past + jnp.arange(s_len)
    mask = kidx[None, :] <= qidx[:, None]
    scale = ATTN_DK ** -0.5

    def one(args):
        qi, ki, vi, pt = args
        kk = jnp.concatenate([cache_k[layer, pt].astype(jnp.float32).reshape(past, N_HEADS, 2, ATTN_DK), ki], axis=0)
        vv = jnp.concatenate([cache_v[layer, pt].astype(jnp.float32).reshape(past, N_HEADS, ATTN_DV), vi], axis=0)
        s = jnp.einsum('qhcd,khcd->hcqk', qi, kk) * scale
        s = jnp.where(mask, s, NEG_INF)
        w = diff_weights(s, lam)
        return jnp.einsum('hqk,khd->qhd', w, vv)

    return lax.map(one, (q, k, v, page_table))


def peer(x, w_q, keys, u_tab, v_tab):
    f32 = jnp.float32
    t = x.shape[0]
    nb = -(-t // PEER_BLOCK)
    xp = jnp.pad(x, ((0, nb * PEER_BLOCK - t), (0, 0))).reshape(nb, PEER_BLOCK, D_MODEL)

    def block(xb):
        q = (xb @ w_q.astype(f32)).reshape(PEER_BLOCK, PEER_HEADS, 2, PEER_DQ // 2)
        s = jnp.einsum('thpd,hpkd->thpk', q, keys.astype(f32))
        s1, i1 = lax.top_k(s[:, :, 0], PEER_TOPK)
        s2, i2 = lax.top_k(s[:, :, 1], PEER_TOPK)
        cand_s = (s1[..., :, None] + s2[..., None, :]).reshape(PEER_BLOCK, PEER_HEADS, PEER_TOPK * PEER_TOPK)
        cand_i = (i1[..., :, None] * N_KEYS + i2[..., None, :]).reshape(PEER_BLOCK, PEER_HEADS, PEER_TOPK * PEER_TOPK)
        top_s, sel = lax.top_k(cand_s, PEER_TOPK)
        idx = jnp.take_along_axis(cand_i, sel, axis=-1)
        g = jax.nn.softmax(top_s, axis=-1)
        act = jax.nn.gelu(jnp.einsum('thkd,td->thk', u_tab[idx].astype(f32), xb))
        return jnp.einsum('thk,thkd->td', g * act, v_tab[idx].astype(f32))

    return lax.map(block, xp).reshape(nb * PEER_BLOCK, D_MODEL)[:t]


def finish_layer(h, y_ssm, o_attn, subln_g, lam_init, w_out, norm2_g, peer_w_q, peer_keys, peer_u, peer_v):
    o = rmsnorm(o_attn, subln_g) * (1.0 - lam_init)
    o = o.reshape(*o.shape[:-2], ATTN_WIDTH)
    h = h + jnp.concatenate([y_ssm, o], axis=-1) @ w_out.astype(jnp.float32)
    n2 = rmsnorm(h, norm2_g)
    return h + peer(n2.reshape(-1, D_MODEL), peer_w_q, peer_keys, peer_u, peer_v).reshape(h.shape)


def setup_inputs(seed: int = 0) -> dict:
    key = jax.random.key(seed)
    ks = jax.random.split(key, 32)
    f32 = jnp.float32

    def nrm(i, shape, scale):
        return jax.random.normal(ks[i], shape, f32) * scale

    n_pages = PAST_LEN // PAGE_SIZE
    n_used = DEC_BATCH * n_pages
    n_pool = n_used + max(1, n_used // 4)
    page_table = jax.random.permutation(ks[0], n_pool)[:n_used].reshape(DEC_BATCH, n_pages).astype(jnp.int32)
    log_dt = math.log(DT_MIN) + jax.random.uniform(ks[1], (DEPTH, SSM_GROUPS), f32) * (math.log(DT_MAX) - math.log(DT_MIN))
    a_im = jnp.pi * jnp.arange(SSM_STATE, dtype=f32)[None, None, :] + nrm(2, (DEPTH, SSM_GROUPS, SSM_STATE), 0.01)
    return {
        "x_prompt": nrm(3, (BATCH, SEQ, D_MODEL), 1.0),
        "x_sample": nrm(4, (DEC_BATCH, DEC_SEQ, D_MODEL), 1.0),
        "cache_k": nrm(5, (DEPTH, n_pool, PAGE_SIZE, N_HEADS, 2, ATTN_DK), 1.0),
        "cache_v": nrm(6, (DEPTH, n_pool, PAGE_SIZE, N_HEADS, ATTN_DV), 1.0),
        "state_ssm_re": nrm(7, (DEPTH, DEC_BATCH, SSM_GROUPS, SSM_STATE), 0.5),
        "state_ssm_im": nrm(8, (DEPTH, DEC_BATCH, SSM_GROUPS, SSM_STATE), 0.5),
        "page_table": page_table,
        "meta_tokens": nrm(9, (N_META, D_MODEL), 1.0),
        "norm1_g": 1.0 + nrm(10, (DEPTH, D_MODEL), 0.01),
        "w_in": nrm(11, (DEPTH, D_MODEL, IN_WIDTH), D_MODEL ** -0.5),
        "q_norm_g": 1.0 + nrm(12, (DEPTH, ATTN_DK), 0.01),
        "k_norm_g": 1.0 + nrm(13, (DEPTH, ATTN_DK), 0.01),
        "lambda_q1": nrm(14, (DEPTH, ATTN_DK), 0.1),
        "lambda_k1": nrm(15, (DEPTH, ATTN_DK), 0.1),
        "lambda_q2": nrm(16, (DEPTH, ATTN_DK), 0.1),
        "lambda_k2": nrm(17, (DEPTH, ATTN_DK), 0.1),
        "subln_g": 1.0 + nrm(18, (DEPTH, ATTN_DV), 0.01),
        "ssm_A_re": -0.5 + nrm(19, (DEPTH, SSM_GROUPS, SSM_STATE), 0.01),
        "ssm_A_im": a_im,
        "ssm_log_dt": log_dt,
        "ssm_B_re": nrm(20, (DEPTH, SSM_GROUPS, SSM_STATE, SSM_GROUP), (2 * SSM_GROUP) ** -0.5),
        "ssm_B_im": nrm(21, (DEPTH, SSM_GROUPS, SSM_STATE, SSM_GROUP), (2 * SSM_GROUP) ** -0.5),
        "ssm_C_re": nrm(22, (DEPTH, SSM_GROUPS, SSM_GROUP, SSM_STATE), (2 * SSM_STATE) ** -0.5),
        "ssm_C_im": nrm(23, (DEPTH, SSM_GROUPS, SSM_GROUP, SSM_STATE), (2 * SSM_STATE) ** -0.5),
        "ssm_D": nrm(24, (DEPTH, SSM_WIDTH), 0.5),
        "w_glu": nrm(25, (DEPTH, SSM_WIDTH, SSM_WIDTH), SSM_WIDTH ** -0.5),
        "b_glu": nrm(26, (DEPTH, SSM_WIDTH), 0.01),
        "w_out": nrm(27, (DEPTH, D_MODEL, D_MODEL), D_MODEL ** -0.5),
        "norm2_g": 1.0 + nrm(28, (DEPTH, D_MODEL), 0.01),
        "peer_w_q": nrm(29, (DEPTH, D_MODEL, PEER_HEADS * PEER_DQ), D_MODEL ** -0.5),
        "peer_keys": nrm(30, (DEPTH, PEER_HEADS, 2, N_KEYS, PEER_DQ // 2), (PEER_DQ // 2) ** -0.5),
        "peer_u": nrm(31, (DEPTH, N_EXPERTS, D_MODEL), D_MODEL ** -0.5),
        "peer_v": jax.random.normal(jax.random.fold_in(key, 99), (DEPTH, N_EXPERTS, D_MODEL), f32) * (PEER_HEADS * PEER_TOPK) ** -0.5,
    }


def reference(x_prompt, x_sample, cache_k, cache_v, state_ssm_re, state_ssm_im, page_table,
              meta_tokens, norm1_g, w_in, q_norm_g, k_norm_g, lambda_q1, lambda_k1, lambda_q2, lambda_k2,
              subln_g, ssm_A_re, ssm_A_im, ssm_log_dt, ssm_B_re, ssm_B_im, ssm_C_re, ssm_C_im, ssm_D,
              w_glu, b_glu, w_out, norm2_g, peer_w_q, peer_keys, peer_u, peer_v):
    f32 = jnp.float32
    bp = x_prompt.shape[0]
    hp = jnp.concatenate([jnp.broadcast_to(meta_tokens.astype(f32)[None], (bp, N_META, D_MODEL)),
                          x_prompt.astype(f32)], axis=1)
    hs = x_sample.astype(f32)
    lp = hp.shape[1]
    ls = hs.shape[1]
    past = page_table.shape[1] * cache_k.shape[2]
    pos_p = jnp.arange(lp, dtype=f32)
    pos_s = past + jnp.arange(ls, dtype=f32)
    zero_state = jnp.zeros((bp, SSM_GROUPS, SSM_STATE), f32)

    k_p_rows, v_p_rows, sre_p, sim_p = [], [], [], []
    k_s_rows, v_s_rows, sre_s, sim_s = [], [], [], []
    for l in range(DEPTH):
        lam_init = lambda_init(l)
        lam = (jnp.exp(jnp.sum(lambda_q1[l].astype(f32) * lambda_k1[l].astype(f32)))
               - jnp.exp(jnp.sum(lambda_q2[l].astype(f32) * lambda_k2[l].astype(f32))) + lam_init)
        ssm_args = (ssm_A_re[l], ssm_A_im[l], ssm_log_dt[l], ssm_B_re[l], ssm_B_im[l],
                    ssm_C_re[l], ssm_C_im[l], ssm_D[l], w_glu[l], b_glu[l])
        peer_args = (peer_w_q[l], peer_keys[l], peer_u[l], peer_v[l])

        up, qp, kp, vp = project(hp, pos_p, norm1_g[l], w_in[l], q_norm_g[l], k_norm_g[l])
        us, qs, ks_, vs = project(hs, pos_s, norm1_g[l], w_in[l], q_norm_g[l], k_norm_g[l])

        ysp, srp, sip = s5_mixer(up, zero_state, zero_state, *ssm_args)
        yss, srs, sis = s5_mixer(us, state_ssm_re[l], state_ssm_im[l], *ssm_args)

        op = attn_prompt(qp, kp, vp, lam)
        os_ = attn_sample(qs, ks_, vs, lam, cache_k, cache_v, page_table, l)

        hp = finish_layer(hp, ysp, op, subln_g[l], lam_init, w_out[l], norm2_g[l], *peer_args)
        hs = finish_layer(hs, yss, os_, subln_g[l], lam_init, w_out[l], norm2_g[l], *peer_args)

        k_p_rows.append(kp.astype(cache_k.dtype))
        v_p_rows.append(vp.astype(cache_v.dtype))
        sre_p.append(srp.astype(state_ssm_re.dtype))
        sim_p.append(sip.astype(state_ssm_im.dtype))
        k_s_rows.append(ks_.astype(cache_k.dtype))
        v_s_rows.append(vs.astype(cache_v.dtype))
        sre_s.append(srs.astype(state_ssm_re.dtype))
        sim_s.append(sis.astype(state_ssm_im.dtype))

    y_prompt = hp[:, N_META:].astype(x_prompt.dtype)
    y_sample = hs.astype(x_sample.dtype)
    return (y_prompt, y_sample,
            jnp.stack(k_p_rows), jnp.stack(v_p_rows), jnp.stack(sre_p), jnp.stack(sim_p),
            jnp.stack(k_s_rows), jnp.stack(v_s_rows), jnp.stack(sre_s), jnp.stack(sim_s))
```

```python
import functools
import math

import jax
import jax.numpy as jnp
from jax import lax
from jax.experimental import pallas as pl
from jax.experimental.pallas import tpu as pltpu

F32 = jnp.float32
BF16 = jnp.bfloat16

D_MODEL = 1024
N_META = 16
SSM_WIDTH = 512
SSM_GROUP = 16
SSM_GROUPS = 32
SSM_STATE = 64
SSM_FLAT = SSM_GROUPS * SSM_STATE
ATTN_WIDTH = 512
ATTN_DV = 128
N_HEADS = 4
ATTN_DK = 64
IN_WIDTH = 2048
ROPE_THETA = 10000.0
N_KEYS = 128
N_EXPERTS = N_KEYS * N_KEYS
PEER_HEADS = 8
PEER_TOPK = 16
EPS = 1e-6
NEG_BIG = -1e30
POS_BIG = 3e38
LAMBDA_INIT = 0.8 - 0.6 * math.exp(-0.3 * 0)

SUBLANES = 8
LANES = 128
VMEM_LIMIT = 56 * 1024 * 1024


def _cparams(sem):
    return pltpu.CompilerParams(dimension_semantics=sem, vmem_limit_bytes=VMEM_LIMIT)


def _gelu(x):
    return 0.5 * x * (1.0 + jnp.tanh(0.7978845608028654 * (x + 0.044715 * (x * x * x))))


def _proj_kernel(x_ref, g1_ref, w_ref, qg_ref, kg_ref, cos_ref, sin_ref, gm_ref,
                 u_ref, k_ref, v_ref, qb_ref, kb_ref, vb_ref):
    x = x_ref[0]
    tm = x.shape[0]
    ms = jnp.mean(x * x, axis=-1, keepdims=True)
    n = (x * lax.rsqrt(ms + EPS) * g1_ref[...]).astype(BF16)
    z = jnp.dot(n, w_ref[...], preferred_element_type=F32)
    u_ref[0] = z[:, :SSM_WIDTH]
    v = z[:, SSM_WIDTH + 2 * ATTN_WIDTH:]
    v_ref[0] = v
    vb_ref[0] = v.astype(BF16)

    cos = jnp.concatenate([cos_ref[...]] * 4, axis=1)
    sin = jnp.concatenate([sin_ref[...]] * 4, axis=1)
    lane = lax.broadcasted_iota(jnp.int32, (tm, ATTN_WIDTH), 1)
    first_half = (lane & (ATTN_DK - 1)) < (ATTN_DK // 2)
    gm = gm_ref[...]

    def norm_rope(t, g):
        sq = t * t
        hi = sq.astype(BF16)
        lo = (sq - hi.astype(F32)).astype(BF16)
        ms64 = (jnp.dot(hi, gm, preferred_element_type=F32)
                + jnp.dot(lo, gm, preferred_element_type=F32))
        tn = t * lax.rsqrt(ms64 + EPS) * g
        swapped = jnp.where(first_half,
                            pltpu.roll(tn, ATTN_WIDTH - ATTN_DK // 2, 1),
                            pltpu.roll(tn, ATTN_DK // 2, 1))
        return tn * cos + swapped * sin

    q = norm_rope(z[:, SSM_WIDTH:SSM_WIDTH + ATTN_WIDTH], qg_ref[...])
    k = norm_rope(z[:, SSM_WIDTH + ATTN_WIDTH:SSM_WIDTH + 2 * ATTN_WIDTH], kg_ref[...])
    k_ref[0] = k
    kb_ref[0] = k.astype(BF16)
    qb_ref[0] = (q * (ATTN_DK ** -0.5)).astype(BF16)


def _project(h, l_out, tm, g1, w_bf, qg, kg, cos_t, sin_t, gm):
    b, lp, _ = h.shape
    nblk = lp // tm
    tok = lambda w: pl.BlockSpec((1, tm, w), lambda i, j: (i, j, 0))
    full = lambda a: pl.BlockSpec(a.shape, lambda i, j: (0,) * a.ndim)
    return pl.pallas_call(
        _proj_kernel,
        grid=(b, nblk),
        in_specs=[tok(D_MODEL), full(g1), full(w_bf), full(qg), full(kg),
                  pl.BlockSpec((tm, LANES), lambda i, j: (j, 0)),
                  pl.BlockSpec((tm, LANES), lambda i, j: (j, 0)), full(gm)],
        out_specs=[tok(SSM_WIDTH), tok(ATTN_WIDTH), tok(ATTN_WIDTH),
                   tok(ATTN_WIDTH), tok(ATTN_WIDTH), tok(ATTN_WIDTH)],
        out_shape=[jax.ShapeDtypeStruct((b, lp, SSM_WIDTH), F32),
                   jax.ShapeDtypeStruct((b, l_out, ATTN_WIDTH), F32),
                   jax.ShapeDtypeStruct((b, l_out, ATTN_WIDTH), F32),
                   jax.ShapeDtypeStruct((b, lp, ATTN_WIDTH), BF16),
                   jax.ShapeDtypeStruct((b, lp, ATTN_WIDTH), BF16),
                   jax.ShapeDtypeStruct((b, lp, ATTN_WIDTH), BF16)],
        compiler_params=_cparams(("parallel", "parallel")),
        name="project",
    )(h, g1, w_bf, qg, kg, cos_t, sin_t, gm)


SSM_CHUNK = 512


def _ssm_kernel(sequential, state_row, state_blk, *refs):
    if sequential:
        (u_ref, bbd_ref, cbd_ref, kc_ref, pc_ref, d_ref, wg_ref, bg_ref,
         y_ref, st_ref, x_sc, carry_sc) = refs
        h0_ref = None
    else:
        (u_ref, bbd_ref, cbd_ref, kc_ref, pc_ref, d_ref, wg_ref, bg_ref, h0_ref,
         y_ref, st_ref, x_sc) = refs
        carry_sc = None
    j = pl.program_id(1)
    u = u_ref[0]
    tm = u.shape[0]
    x_sc[...] = jnp.dot(u.astype(BF16), bbd_ref[...], preferred_element_type=F32)

    if sequential:
        @pl.when(j == 0)
        def _():
            carry_sc[...] = jnp.zeros_like(carry_sc)

    def tile_body(i, carry):
        r0 = pl.multiple_of(i * SUBLANES, SUBLANES)
        for c in range(0, SSM_FLAT, SSM_CHUNK):
            re = slice(c, c + SSM_CHUNK)
            im = slice(SSM_FLAT + c, SSM_FLAT + c + SSM_CHUNK)
            xr = x_sc[pl.ds(r0, SUBLANES), re]
            xi = x_sc[pl.ds(r0, SUBLANES), im]
            for di, d in enumerate((1, 2, 4)):
                cr = kc_ref[di, 0, :, re]
                ci = kc_ref[di, 1, :, re]
                rr = pltpu.roll(xr, d, 0)
                ri = pltpu.roll(xi, d, 0)
                xr, xi = xr + cr * rr - ci * ri, xi + cr * ri + ci * rr
            if sequential:
                car_r = carry_sc[:, re]
                car_i = carry_sc[:, im]
            else:
                car_r = h0_ref[0, pl.ds(r0, SUBLANES), re]
                car_i = h0_ref[0, pl.ds(r0, SUBLANES), im]
            pr = pc_ref[0, :, re]
            pi = pc_ref[1, :, re]
            xr, xi = xr + pr * car_r - pi * car_i, xi + pr * car_i + pi * car_r
            x_sc[pl.ds(r0, SUBLANES), re] = xr
            x_sc[pl.ds(r0, SUBLANES), im] = xi
            if sequential:
                carry_sc[:, re] = jnp.broadcast_to(xr[SUBLANES - 1:, :], (SUBLANES, SSM_CHUNK))
                carry_sc[:, im] = jnp.broadcast_to(xi[SUBLANES - 1:, :], (SUBLANES, SSM_CHUNK))
        return carry

    lax.fori_loop(0, tm // SUBLANES, tile_body, 0)

    if sequential:
        @pl.when(j == state_blk)
        def _():
            st_ref[0] = x_sc[state_row:state_row + SUBLANES, :]
    else:
        st_ref[0] = x_sc[...]

    y = jnp.dot(x_sc[...].astype(BF16), cbd_ref[...], preferred_element_type=F32) + d_ref[...] * u
    y = _gelu(y)
    gate = jnp.dot(y.astype(BF16), wg_ref[...], preferred_element_type=F32) + bg_ref[...]
    y_ref[0] = (y * (1.0 / (1.0 + jnp.exp(-gate)))).astype(BF16)


def _ssm(u, tm, last_token, consts, h0=None):
    bbd, cbd, kc, pc, dsk, wg, bg = consts
    b, lp, _ = u.shape
    nblk = lp // tm
    sequential = h0 is None
    full = lambda a: pl.BlockSpec(a.shape, lambda i, j: (0,) * a.ndim)
    in_specs = [pl.BlockSpec((1, tm, SSM_WIDTH), lambda i, j: (i, j, 0)),
                full(bbd), full(cbd), full(kc), full(pc), full(dsk), full(wg), full(bg)]
    args = [u, bbd, cbd, kc, pc, dsk, wg, bg]
    scratch = [pltpu.VMEM((tm, 2 * SSM_FLAT), F32)]
    if sequential:
        state_blk = last_token // tm
        state_row = (last_token % tm) // SUBLANES * SUBLANES
        st_spec = pl.BlockSpec((1, SUBLANES, 2 * SSM_FLAT), lambda i, j: (i, 0, 0))
        st_shape = jax.ShapeDtypeStruct((b, SUBLANES, 2 * SSM_FLAT), F32)
        scratch.append(pltpu.VMEM((SUBLANES, 2 * SSM_FLAT), F32))
        sem = ("parallel", "arbitrary")
    else:
        state_blk = state_row = 0
        in_specs.append(pl.BlockSpec((1, tm, 2 * SSM_FLAT), lambda i, j: (i, j, 0)))
        args.append(jnp.repeat(h0, SUBLANES, axis=1))
        st_spec = pl.BlockSpec((1, tm, 2 * SSM_FLAT), lambda i, j: (i, j, 0))
        st_shape = jax.ShapeDtypeStruct((b, lp, 2 * SSM_FLAT), F32)
        sem = ("parallel", "parallel")
    return pl.pallas_call(
        functools.partial(_ssm_kernel, sequential, state_row, state_blk),
        grid=(b, nblk),
        in_specs=in_specs,
        out_specs=[pl.BlockSpec((1, tm, SSM_WIDTH), lambda i, j: (i, j, 0)), st_spec],
        out_shape=[jax.ShapeDtypeStruct((b, lp, SSM_WIDTH), BF16), st_shape],
        scratch_shapes=scratch,
        compiler_params=_cparams(sem),
        name="s5_prompt" if sequential else "s5_sample",
    )(*args)


def _ssm_consts(a_re, a_im, log_dt, b_re, b_im, c_re, c_im, d_skip, w_glu, b_glu):
    dt = jnp.exp(log_dt.astype(F32))[:, None]
    a_re = a_re.astype(F32)
    a_im = a_im.astype(F32)
    mag = jnp.exp(dt * a_re)
    ab_re = mag * jnp.cos(dt * a_im)
    ab_im = mag * jnp.sin(dt * a_im)
    den = a_re * a_re + a_im * a_im
    z_re = ((ab_re - 1.0) * a_re + ab_im * a_im) / den
    z_im = (ab_im * a_re - (ab_re - 1.0) * a_im) / den
    b_re = b_re.astype(F32)
    b_im = b_im.astype(F32)
    bb_re = z_re[..., None] * b_re - z_im[..., None] * b_im
    bb_im = z_re[..., None] * b_im + z_im[..., None] * b_re
    eye = jnp.eye(SSM_GROUPS, dtype=F32)
    bbd = jnp.concatenate(
        [jnp.einsum('gnc,gh->gchn', bb_re, eye).reshape(SSM_WIDTH, SSM_FLAT),
         jnp.einsum('gnc,gh->gchn', bb_im, eye).reshape(SSM_WIDTH, SSM_FLAT)], axis=1).astype(BF16)
    cbd = jnp.concatenate(
        [jnp.einsum('gcn,gh->gnhc', c_re.astype(F32), eye).reshape(SSM_FLAT, SSM_WIDTH),
         -jnp.einsum('gcn,gh->gnhc', c_im.astype(F32), eye).reshape(SSM_FLAT, SSM_WIDTH)], axis=0).astype(BF16)

    def power(p):
        m = jnp.exp(p * dt * a_re)
        return (m * jnp.cos(p * dt * a_im)).reshape(-1), (m * jnp.sin(p * dt * a_im)).reshape(-1)

    row = jnp.arange(SUBLANES)[:, None]
    kc = []
    for d in (1, 2, 4):
        pr, pi = power(float(d))
        kc.append(jnp.stack([jnp.where(row >= d, pr[None, :], 0.0), jnp.where(row >= d, pi[None, :], 0.0)]))
    kc = jnp.stack(kc)
    rows = [power(float(s + 1)) for s in range(SUBLANES)]
    pc = jnp.stack([jnp.stack([r[0] for r in rows]), jnp.stack([r[1] for r in rows])])
    return (bbd, cbd, kc, pc, d_skip.astype(F32)[None, :], w_glu.astype(BF16), b_glu.astype(F32)[None, :])


def _lambda_value(lamp_ref):
    lp = lamp_ref[...]
    s1 = jnp.sum(lp[0:1] * lp[1:2], axis=1, keepdims=True)
    s2 = jnp.sum(lp[2:3] * lp[3:4], axis=1, keepdims=True)
    return jnp.exp(s1) - jnp.exp(s2) + LAMBDA_INIT


def _stack_sub_queries(q):
    lane = lax.broadcasted_iota(jnp.int32, q.shape, 1)
    zero = jnp.zeros_like(q)
    return jnp.concatenate([jnp.where(lane < ATTN_DK, q, zero), jnp.where(lane >= ATTN_DK, q, zero)], axis=0)


def _online_softmax_step(s, v, m_sc, l_sc, acc_sc):
    tk = s.shape[1]
    m_prev = m_sc[...]
    m_new = jnp.maximum(m_prev, jnp.max(s, axis=1, keepdims=True))
    p = jnp.exp(s - pltpu.repeat(m_new, tk // LANES, 1))
    alpha = jnp.exp(m_prev - m_new)
    l_sc[...] = alpha * l_sc[...] + jnp.sum(p, axis=1, keepdims=True)
    acc_sc[...] = alpha * acc_sc[...] + jnp.dot(p.astype(BF16), v, preferred_element_type=F32)
    m_sc[...] = m_new


def _attn_prompt_kernel(tq, tk, lamp_ref, q_ref, k_ref, v_ref, o_ref, q2_sc, m_sc, l_sc, acc_sc):
    iq = pl.program_id(2)
    q2_sc[...] = _stack_sub_queries(q_ref[0])
    m_sc[...] = jnp.full_like(m_sc, NEG_BIG)
    l_sc[...] = jnp.zeros_like(l_sc)
    acc_sc[...] = jnp.zeros_like(acc_sc)

    def scores(ik):
        k0 = pl.multiple_of(ik * tk, tk)
        s = lax.dot_general(q2_sc[...], k_ref[0, pl.ds(k0, tk), :], (((1,), (1,)), ((), ())),
                            preferred_element_type=F32)
        return s, v_ref[0, pl.ds(k0, tk), :]

    n_full = (iq * tq) // tk
    n_kv = ((iq + 1) * tq + tk - 1) // tk

    def full_body(ik, c):
        s, v = scores(ik)
        _online_softmax_step(s, v, m_sc, l_sc, acc_sc)
        return c

    lax.fori_loop(0, n_full, full_body, 0)

    def diag_body(ik, c):
        s, v = scores(ik)
        row = lax.broadcasted_iota(jnp.int32, s.shape, 0)
        col = lax.broadcasted_iota(jnp.int32, s.shape, 1)
        qpos = iq * tq + jnp.where(row >= tq, row - tq, row)
        s = jnp.where(ik * tk + col <= qpos, s, NEG_BIG)
        _online_softmax_step(s, v, m_sc, l_sc, acc_sc)
        return c

    lax.fori_loop(n_full, n_kv, diag_body, 0)

    lam = _lambda_value(lamp_ref)
    o = acc_sc[...] / l_sc[...]
    o_ref[0] = o[:tq] - lam * o[tq:]


def _attn_prompt(lamp, qb, kb, vb, tq, tk):
    b, lp, _ = qb.shape
    return pl.pallas_call(
        functools.partial(_attn_prompt_kernel, tq, tk),
        grid=(b, N_HEADS, lp // tq),
        in_specs=[pl.BlockSpec(lamp.shape, lambda i, h, j: (0, 0)),
                  pl.BlockSpec((1, tq, ATTN_DV), lambda i, h, j: (i, j, h)),
                  pl.BlockSpec((1, lp, ATTN_DV), lambda i, h, j: (i, 0, h)),
                  pl.BlockSpec((1, lp, ATTN_DV), lambda i, h, j: (i, 0, h))],
        out_specs=pl.BlockSpec((1, tq, ATTN_DV), lambda i, h, j: (i, j, h)),
        out_shape=jax.ShapeDtypeStruct((b, lp, ATTN_WIDTH), F32),
        scratch_shapes=[pltpu.VMEM((2 * tq, ATTN_DV), BF16),
                        pltpu.VMEM((2 * tq, LANES), F32),
                        pltpu.VMEM((2 * tq, LANES), F32),
                        pltpu.VMEM((2 * tq, ATTN_DV), F32)],
        compiler_params=_cparams(("parallel", "parallel", "parallel")),
        name="attn_prompt",
    )(lamp, qb, kb, vb)


def _attn_sample_kernel(n_pages, s_len, pt_ref, lamp_ref, q_ref, kn_ref, vn_ref, *refs):
    k_refs = refs[:n_pages]
    v_refs = refs[n_pages:2 * n_pages]
    o_ref, q2_sc, m_sc, l_sc, acc_sc = refs[2 * n_pages:]
    j = pl.program_id(1)
    rows = 2 * s_len

    @pl.when(j == 0)
    def _():
        for h in range(N_HEADS):
            q2_sc[h] = _stack_sub_queries(q_ref[0, :, h * ATTN_DV:(h + 1) * ATTN_DV])
        m_sc[...] = jnp.full_like(m_sc, NEG_BIG)
        l_sc[...] = jnp.zeros_like(l_sc)
        acc_sc[...] = jnp.zeros_like(acc_sc)

    def step(h, kh, vh, mask=None):
        s = lax.dot_general(q2_sc[h], kh, (((1,), (1,)), ((), ())), preferred_element_type=F32)
        if mask is not None:
            s = jnp.where(mask, s, NEG_BIG)
        _online_softmax_step(s, vh, m_sc.at[h], l_sc.at[h], acc_sc.at[h])

    for h in range(N_HEADS):
        hs = slice(h * ATTN_DV, (h + 1) * ATTN_DV)
        kh = jnp.concatenate([r[0, :, hs] for r in k_refs], axis=0).astype(BF16)
        vh = jnp.concatenate([r[0, :, hs] for r in v_refs], axis=0).astype(BF16)
        step(h, kh, vh)

    @pl.when(j == pl.num_programs(1) - 1)
    def _():
        lam = _lambda_value(lamp_ref)
        pad = jnp.zeros((LANES - s_len, ATTN_DV), BF16)
        row = lax.broadcasted_iota(jnp.int32, (rows, LANES), 0)
        col = lax.broadcasted_iota(jnp.int32, (rows, LANES), 1)
        mask = col <= jnp.where(row >= s_len, row - s_len, row)
        for h in range(N_HEADS):
            hs = slice(h * ATTN_DV, (h + 1) * ATTN_DV)
            step(h, jnp.concatenate([kn_ref[0, :, hs], pad], axis=0),
                 jnp.concatenate([vn_ref[0, :, hs], pad], axis=0), mask)
            o = acc_sc[h] / l_sc[h]
            o_ref[0, :, hs] = o[:s_len] - lam * o[s_len:]


def _attn_sample(lamp, qb, kb, vb, cache_k, cache_v, page_table, n_pages):
    db, s_len, _ = qb.shape
    n_past_pages = page_table.shape[1]
    page = cache_k.shape[1]
    steps = n_past_pages // n_pages
    tok = pl.BlockSpec((1, s_len, ATTN_WIDTH), lambda i, j, pt: (i, 0, 0))

    def page_spec(p):
        return pl.BlockSpec((1, page, ATTN_WIDTH), lambda i, j, pt: (pt[i, j * n_pages + p], 0, 0))

    rows = 2 * s_len
    return pl.pallas_call(
        functools.partial(_attn_sample_kernel, n_pages, s_len),
        grid_spec=pltpu.PrefetchScalarGridSpec(
            num_scalar_prefetch=1,
            grid=(db, steps),
            in_specs=[pl.BlockSpec(lamp.shape, lambda i, j, pt: (0, 0)), tok, tok, tok]
                     + [page_spec(p) for p in range(n_pages)] * 2,
            out_specs=pl.BlockSpec((1, s_len, ATTN_WIDTH), lambda i, j, pt: (i, 0, 0)),
            scratch_shapes=[pltpu.VMEM((N_HEADS, rows, ATTN_DV), BF16),
                            pltpu.VMEM((N_HEADS, rows, LANES), F32),
                            pltpu.VMEM((N_HEADS, rows, LANES), F32),
                            pltpu.VMEM((N_HEADS, rows, ATTN_DV), F32)]),
        out_shape=jax.ShapeDtypeStruct((db, s_len, ATTN_WIDTH), F32),
        compiler_params=_cparams(("parallel", "arbitrary")),
        name="attn_sample",
    )(page_table, lamp, qb, kb, vb, *([cache_k] * n_pages), *([cache_v] * n_pages))


def _finish_kernel(h_ref, ys_ref, o_ref, sg_ref, wout_ref, g2_ref, wq_ref, keys_ref,
                   h1_ref, n2_ref, st_ref):
    o = o_ref[0]
    parts = []
    for h in range(N_HEADS):
        oh = o[:, h * ATTN_DV:(h + 1) * ATTN_DV]
        parts.append(oh * lax.rsqrt(jnp.mean(oh * oh, axis=-1, keepdims=True) + EPS))
    on = jnp.concatenate(parts, axis=1) * sg_ref[...] * (1.0 - LAMBDA_INIT)
    mix = jnp.concatenate([ys_ref[0], on.astype(BF16)], axis=1)
    h1 = h_ref[0] + jnp.dot(mix, wout_ref[...], preferred_element_type=F32)
    h1_ref[0] = h1
    n2 = (h1 * lax.rsqrt(jnp.mean(h1 * h1, axis=-1, keepdims=True) + EPS) * g2_ref[...]).astype(BF16)
    n2_ref[0] = n2
    qt = lax.dot_general(wq_ref[...], n2, (((1,), (1,)), ((), ())), preferred_element_type=F32)
    for hp in range(2 * PEER_HEADS):
        st_ref[0, hp] = jnp.dot(keys_ref[hp], qt[hp * N_KEYS:(hp + 1) * N_KEYS].astype(BF16),
                                preferred_element_type=F32)


def _finish(h, ys, o, tm, sg, wout, g2, wq_t, keys):
    b, lp, _ = h.shape
    nblk = lp // tm
    tok = lambda w: pl.BlockSpec((1, tm, w), lambda i, j: (i, j, 0))
    full = lambda a: pl.BlockSpec(a.shape, lambda i, j: (0,) * a.ndim)
    return pl.pallas_call(
        _finish_kernel,
        grid=(b, nblk),
        in_specs=[tok(D_MODEL), tok(SSM_WIDTH), tok(ATTN_WIDTH), full(sg), full(wout), full(g2),
                  full(wq_t), full(keys)],
        out_specs=[tok(D_MODEL), tok(D_MODEL),
                   pl.BlockSpec((1, 2 * PEER_HEADS, N_KEYS, tm), lambda i, j: (i, 0, 0, j))],
        out_shape=[jax.ShapeDtypeStruct((b, lp, D_MODEL), F32),
                   jax.ShapeDtypeStruct((b, lp, D_MODEL), BF16),
                   jax.ShapeDtypeStruct((b, 2 * PEER_HEADS, N_KEYS, lp), F32)],
        compiler_params=_cparams(("parallel", "parallel")),
        name="finish",
    )(h, ys, o, sg, wout, g2, wq_t, keys)


def _cmpx(v, i, j):
    hi = jnp.maximum(v[i], v[j])
    lo = jnp.minimum(v[i], v[j])
    v[i], v[j] = hi, lo


def _bitonic_merge_desc(v):
    n = len(v)
    v = list(v)
    d = n // 2
    while d >= 1:
        for i in range(n):
            if (i % (2 * d)) < d:
                _cmpx(v, i, i + d)
        d //= 2
    return v


def _sort_desc(v):
    n = len(v)
    if n == 1:
        return list(v)
    a = _sort_desc(v[:n // 2])
    b = _sort_desc(v[n // 2:])
    return _bitonic_merge_desc(a + b[::-1])


def _merge_top(t, s):
    n = len(t)
    m = list(t)
    for r in range(n):
        q = n - 1 - r
        if q < len(s):
            m[r] = jnp.maximum(t[r], s[q])
    return _bitonic_merge_desc(m)


def _peer_select_kernel(s_ref, tau_ref, w1_ref, e2_ref):
    tb = s_ref.shape[-1]
    sub = lax.broadcasted_iota(jnp.int32, (SUBLANES, LANES), 0)
    k = PEER_TOPK

    def lane_group(ln):
        tops = []
        for hp in range(2 * PEER_HEADS):
            col = _sort_desc([s_ref[0, hp, r * SUBLANES:(r + 1) * SUBLANES, ln] for r in range(N_KEYS // SUBLANES)])
            for shift in (4, 2, 1):
                other = [pltpu.roll(c, shift, 0) for c in col]
                col = _merge_top(col, other)
            tops.append(col)

        def pack(lists):
            out = []
            for r in range(k):
                x = lists[0][r]
                for h in range(1, PEER_HEADS):
                    x = jnp.where(sub == h, lists[h][r], x)
                out.append(x)
            return out

        a = pack([tops[2 * h] for h in range(PEER_HEADS)])
        b = pack([tops[2 * h + 1] for h in range(PEER_HEADS)])
        t = [a[0] + b[q] for q in range(k)]
        for i in range(1, k // 2):
            t = _merge_top(t, [a[i] + b[q] for q in range(k // (i + 1))])
        t = _merge_top(t, [a[i] + b[0] for i in range(k // 2, k)])
        theta = t[k - 1]
        z = jnp.ones_like(theta)
        for r in range(1, k):
            z = z + jnp.exp(t[r] - t[0])
        inv_z = 1.0 / z
        taus = []
        for r in range(k):
            tau = jnp.full_like(theta, POS_BIG)
            for q in range(k // (r + 1)):
                tau = jnp.where(a[r] + b[q] >= theta, b[q], tau)
            taus.append(tau)

        for h in range(PEER_HEADS):
            bc = lambda x: jnp.broadcast_to(x[h:h + 1, :], (SUBLANES, LANES))
            a_h = tops[2 * h]
            tau_h = [bc(x) for x in taus]
            inv_z_h = bc(inv_z)
            a0 = a_h[0]
            b0 = tops[2 * h + 1][0]
            for r in range(N_KEYS // SUBLANES):
                rs = slice(r * SUBLANES, (r + 1) * SUBLANES)
                s1 = s_ref[0, 2 * h, rs, ln]
                tau = jnp.full_like(s1, POS_BIG)
                for q in range(k):
                    tau = jnp.where(s1 == a_h[q], tau_h[q], tau)
                tau_ref[0, h, rs, ln] = tau
                w1_ref[0, h, rs, ln] = jnp.exp(s1 - a0) * inv_z_h
                e2_ref[0, h, rs, ln] = jnp.exp(s_ref[0, 2 * h + 1, rs, ln] - b0)

    for g in range(tb // LANES):
        lane_group(slice(g * LANES, (g + 1) * LANES))


def _peer_select(st):
    tb = LANES
    b, _, _, lp = st.shape
    out = jax.ShapeDtypeStruct((b, PEER_HEADS, N_KEYS, lp), F32)
    spec = pl.BlockSpec((1, PEER_HEADS, N_KEYS, tb), lambda i, j: (i, 0, 0, j))
    return pl.pallas_call(
        _peer_select_kernel,
        grid=(b, lp // tb),
        in_specs=[pl.BlockSpec((1, 2 * PEER_HEADS, N_KEYS, tb), lambda i, j: (i, 0, 0, j))],
        out_specs=[spec, spec, spec],
        out_shape=[out, out, out],
        compiler_params=_cparams(("parallel", "parallel")),
        name="peer_select",
    )(st)


def _peer_dense_kernel(ec, n2_ref, u_ref, vt_ref, s2_ref, e2_ref, tau_ref, w1_ref, h1_ref,
                       out_ref, a_sc, acc_sc):
    c = pl.program_id(2)
    tb = n2_ref.shape[1]

    @pl.when(c == 0)
    def _():
        acc_sc[...] = jnp.zeros_like(acc_sc)

    a_sc[...] = lax.dot_general(u_ref[...], n2_ref[0], (((1,), (1,)), ((), ())), preferred_element_type=F32)

    for il in range(ec // N_KEYS):
        for cc in range(tb // LANES):
            ln = slice(cc * LANES, (cc + 1) * LANES)
            tau = [jnp.broadcast_to(tau_ref[0, h, il:il + 1, ln], (SUBLANES, LANES)) for h in range(PEER_HEADS)]
            w1 = [jnp.broadcast_to(w1_ref[0, h, il:il + 1, ln], (SUBLANES, LANES)) for h in range(PEER_HEADS)]
            for jv in range(N_KEYS // SUBLANES):
                rs = slice(jv * SUBLANES, (jv + 1) * SUBLANES)
                gate = jnp.zeros((SUBLANES, LANES), F32)
                for h in range(PEER_HEADS):
                    gate = gate + jnp.where(s2_ref[0, h, 0, rs, ln] >= tau[h], e2_ref[0, h, rs, ln] * w1[h], 0.0)
                rows = slice(il * N_KEYS + jv * SUBLANES, il * N_KEYS + (jv + 1) * SUBLANES)
                a_sc[rows, ln] = gate * _gelu(a_sc[rows, ln])

    acc_sc[...] += jnp.dot(vt_ref[...], a_sc[...].astype(BF16), preferred_element_type=F32)

    @pl.when(c == pl.num_programs(2) - 1)
    def _():
        out_ref[0] = h1_ref[0] + acc_sc[...].T


def _peer_dense(n2, u_bf, vt_bf, st, e2, tau, w1, h1, tb, ec):
    b, lp, _ = n2.shape
    rows = ec // N_KEYS
    st5 = st.reshape(b, PEER_HEADS, 2, N_KEYS, lp)
    return pl.pallas_call(
        functools.partial(_peer_dense_kernel, ec),
        grid=(b, lp // tb, N_EXPERTS // ec),
        in_specs=[pl.BlockSpec((1, tb, D_MODEL), lambda i, j, c: (i, j, 0)),
                  pl.BlockSpec((ec, D_MODEL), lambda i, j, c: (c, 0)),
                  pl.BlockSpec((D_MODEL, ec), lambda i, j, c: (0, c)),
                  pl.BlockSpec((1, PEER_HEADS, 1, N_KEYS, tb), lambda i, j, c: (i, 0, 1, 0, j)),
                  pl.BlockSpec((1, PEER_HEADS, N_KEYS, tb), lambda i, j, c: (i, 0, 0, j)),
                  pl.BlockSpec((1, PEER_HEADS, rows, tb), lambda i, j, c: (i, 0, c, j)),
                  pl.BlockSpec((1, PEER_HEADS, rows, tb), lambda i, j, c: (i, 0, c, j)),
                  pl.BlockSpec((1, tb, D_MODEL), lambda i, j, c: (i, j, 0))],
        out_specs=pl.BlockSpec((1, tb, D_MODEL), lambda i, j, c: (i, j, 0)),
        out_shape=jax.ShapeDtypeStruct((b, lp, D_MODEL), F32),
        scratch_shapes=[pltpu.VMEM((ec, tb), F32), pltpu.VMEM((D_MODEL, tb), F32)],
        compiler_params=_cparams(("parallel", "parallel", "arbitrary")),
        name="peer_dense",
    )(n2, u_bf, vt_bf, st5, e2, tau, w1, h1)


def _rope_tables(pos):
    half = ATTN_DK // 2
    inv_freq = ROPE_THETA ** (-jnp.arange(half, dtype=F32) * 2.0 / ATTN_DK)
    ang = pos[:, None] * inv_freq[None, :]
    cos = jnp.cos(ang)
    sin = jnp.sin(ang)
    return (jnp.concatenate([cos, cos, cos, cos], axis=1),
            jnp.concatenate([-sin, sin, -sin, sin], axis=1))


def _round_up(x, m):
    return (x + m - 1) // m * m


def _layer(h, l_real, pos, weights, tm, tb, ec, attend):
    (g1, w_in, qg, kg, gm, ssm_c, sg, wout, g2, wq_t, keys, u_bf, vt_bf) = weights
    cos_t, sin_t = _rope_tables(pos)
    u, k, v, qb, kb, vb = _project(h, l_real, tm, g1, w_in, qg, kg, cos_t, sin_t, gm)
    o, ys, st = attend(u, qb, kb, vb, ssm_c)
    h1, n2, sc = _finish(h, ys, o, tb, sg, wout, g2, wq_t, keys)
    tau, w1, e2 = _peer_select(sc)
    out = _peer_dense(n2, u_bf, vt_bf, sc, e2, tau, w1, h1, tb, ec)
    return out, k, v, st


def kernel(x_prompt, x_sample, cache_k, cache_v, state_ssm_re, state_ssm_im, page_table, meta_tokens, norm1_g, w_in, q_norm_g, k_norm_g, lambda_q1, lambda_k1, lambda_q2, lambda_k2, subln_g, ssm_A_re, ssm_A_im, ssm_log_dt, ssm_B_re, ssm_B_im, ssm_C_re, ssm_C_im, ssm_D, w_glu, b_glu, w_out, norm2_g, peer_w_q, peer_keys, peer_u, peer_v):
    assert w_in.shape[0] == 1, "single-layer trunk"
    bp, seq, _ = x_prompt.shape
    db, ds, _ = x_sample.shape
    lp_real = seq + N_META
    assert lp_real % SUBLANES == 0 and ds == SUBLANES
    tm_p, tq, tk, tb, ec = 768, 256, 512, 512, 1024
    lp = _round_up(lp_real, math.lcm(tm_p, tk, tb))
    past = page_table.shape[1] * cache_k.shape[2]

    gm = jnp.kron(jnp.eye(ATTN_WIDTH // ATTN_DK, dtype=F32),
                  jnp.full((ATTN_DK, ATTN_DK), 1.0 / ATTN_DK, F32)).astype(BF16)
    lamp = jnp.stack([lambda_q1[0], lambda_k1[0], lambda_q2[0], lambda_k2[0]]).astype(F32)
    ssm_c = _ssm_consts(ssm_A_re[0], ssm_A_im[0], ssm_log_dt[0], ssm_B_re[0], ssm_B_im[0],
                        ssm_C_re[0], ssm_C_im[0], ssm_D[0], w_glu[0], b_glu[0])
    weights = (norm1_g[0].astype(F32)[None, :], w_in[0].astype(BF16),
               jnp.tile(q_norm_g[0].astype(F32), ATTN_WIDTH // ATTN_DK)[None, :],
               jnp.tile(k_norm_g[0].astype(F32), ATTN_WIDTH // ATTN_DK)[None, :],
               gm, ssm_c,
               jnp.tile(subln_g[0].astype(F32), N_HEADS)[None, :], w_out[0].astype(BF16),
               norm2_g[0].astype(F32)[None, :], peer_w_q[0].T.astype(BF16),
               peer_keys[0].reshape(2 * PEER_HEADS, N_KEYS, N_KEYS).astype(BF16),
               peer_u[0].astype(BF16), peer_v[0].T.astype(BF16))

    hp = jnp.concatenate([jnp.broadcast_to(meta_tokens.astype(F32)[None], (bp, N_META, D_MODEL)),
                          x_prompt.astype(F32),
                          jnp.zeros((bp, lp - lp_real, D_MODEL), F32)], axis=1)

    def attend_prompt(u, qb, kb, vb, consts):
        ys, st = _ssm(u, tm_p, lp_real - 1, consts)
        return _attn_prompt(lamp, qb, kb, vb, tq, tk), ys, st

    hp_out, k_p, v_p, st_p = _layer(hp, lp_real, jnp.arange(lp, dtype=F32), weights, tm_p, tb, ec, attend_prompt)

    n_tok = db * ds
    hs = x_sample.astype(F32).reshape(1, n_tok, D_MODEL)
    h0 = jnp.concatenate([state_ssm_re[0].reshape(db, SSM_FLAT), state_ssm_im[0].reshape(db, SSM_FLAT)],
                         axis=1).astype(F32)[None]
    ck = cache_k[0].reshape(cache_k.shape[1], cache_k.shape[2], ATTN_WIDTH)
    cv = cache_v[0].reshape(cache_v.shape[1], cache_v.shape[2], ATTN_WIDTH)

    def attend_sample(u, qb, kb, vb, consts):
        ys, st = _ssm(u, 256, None, consts, h0=h0)
        seqs = lambda a: a.reshape(db, ds, ATTN_WIDTH)
        o = _attn_sample(lamp, seqs(qb), seqs(kb), seqs(vb), ck, cv, page_table, 8)
        return o.reshape(1, n_tok, ATTN_WIDTH), ys, st

    pos_s = jnp.tile(past + jnp.arange(ds, dtype=F32), db)
    hs_out, k_s, v_s, st_s = _layer(hs, n_tok, pos_s, weights, 512, tb, ec, attend_sample)

    y_prompt = hp_out[:, N_META:lp_real].astype(x_prompt.dtype)
    y_sample = hs_out.reshape(db, ds, D_MODEL).astype(x_sample.dtype)
    row = (lp_real - 1) % SUBLANES
    kd, vd, sd = cache_k.dtype, cache_v.dtype, state_ssm_re.dtype
    return (y_prompt, y_sample,
            k_p.reshape(1, bp, lp_real, N_HEADS, 2, ATTN_DK).astype(kd),
            v_p.reshape(1, bp, lp_real, N_HEADS, ATTN_DV).astype(vd),
            st_p[:, row, :SSM_FLAT].reshape(1, bp, SSM_GROUPS, SSM_STATE).astype(sd),
            st_p[:, row, SSM_FLAT:].reshape(1, bp, SSM_GROUPS, SSM_STATE).astype(state_ssm_im.dtype),
            k_s.reshape(1, db, ds, N_HEADS, 2, ATTN_DK).astype(kd),
            v_s.reshape(1, db, ds, N_HEADS, ATTN_DV).astype(vd),
            st_s[0, ds - 1::ds, :SSM_FLAT].reshape(1, db, SSM_GROUPS, SSM_STATE).astype(sd),
            st_s[0, ds - 1::ds, SSM_FLAT:].reshape(1, db, SSM_GROUPS, SSM_STATE).astype(state_ssm_im.dtype))
```

```python
import functools
import math

import jax
import jax.numpy as jnp
from jax import lax
from jax.experimental import pallas as pl
from jax.experimental.pallas import tpu as pltpu

F32 = jnp.float32
BF16 = jnp.bfloat16

D_MODEL = 1024
N_META = 16
SSM_WIDTH = 512
SSM_GROUP = 16
SSM_GROUPS = 32
SSM_STATE = 64
SSM_FLAT = SSM_GROUPS * SSM_STATE
ATTN_WIDTH = 512
ATTN_DV = 128
N_HEADS = 4
ATTN_DK = 64
IN_WIDTH = 2048
ROPE_THETA = 10000.0
N_KEYS = 128
N_EXPERTS = N_KEYS * N_KEYS
PEER_HEADS = 8
PEER_TOPK = 16
PEER_SUB = 2 * N_KEYS
EPS = 1e-6
NEG_BIG = -1e30
POS_BIG = 3e38
LAMBDA_INIT = 0.8 - 0.6 * math.exp(-0.3 * 0)
Q_SCALE = ATTN_DK ** -0.5 * math.log2(math.e)

SUBLANES = 8
LANES = 128
VMEM_LIMIT = 56 * 1024 * 1024


def _cparams(sem):
    return pltpu.CompilerParams(dimension_semantics=sem, vmem_limit_bytes=VMEM_LIMIT)


def _gelu(x):
    return 0.5 * x * (1.0 + jnp.tanh(0.7978845608028654 * (x + 0.044715 * (x * x * x))))


def _proj_kernel(x_ref, g1_ref, w_ref, qg_ref, kg_ref, cos_ref, sin_ref, gm_ref,
                 u_ref, k_ref, v_ref, qb_ref, kb_ref, vb_ref):
    x = x_ref[0]
    tm = x.shape[0]
    ms = jnp.mean(x * x, axis=-1, keepdims=True)
    n = (x * lax.rsqrt(ms + EPS) * g1_ref[...]).astype(BF16)
    z = jnp.dot(n, w_ref[...], preferred_element_type=F32)
    u_ref[0] = z[:, :SSM_WIDTH]
    v = z[:, SSM_WIDTH + 2 * ATTN_WIDTH:]
    v_ref[0] = v
    vb = v.astype(BF16)
    ones = jnp.ones((tm, ATTN_DV), BF16)
    vb_ref[0] = jnp.concatenate([t for h in range(N_HEADS) for t in (vb[:, h * ATTN_DV:(h + 1) * ATTN_DV], ones)], axis=1)

    cos = jnp.concatenate([cos_ref[...]] * 4, axis=1)
    sin = jnp.concatenate([sin_ref[...]] * 4, axis=1)
    lane = lax.broadcasted_iota(jnp.int32, (tm, ATTN_WIDTH), 1)
    first_half = (lane & (ATTN_DK - 1)) < (ATTN_DK // 2)
    gm = gm_ref[...]

    def norm_rope(t, g):
        sq = t * t
        hi = sq.astype(BF16)
        lo = (sq - hi.astype(F32)).astype(BF16)
        ms64 = (jnp.dot(hi, gm, preferred_element_type=F32)
                + jnp.dot(lo, gm, preferred_element_type=F32))
        tn = t * lax.rsqrt(ms64 + EPS) * g
        swapped = jnp.where(first_half,
                            pltpu.roll(tn, ATTN_WIDTH - ATTN_DK // 2, 1),
                            pltpu.roll(tn, ATTN_DK // 2, 1))
        return tn * cos + swapped * sin

    q = norm_rope(z[:, SSM_WIDTH:SSM_WIDTH + ATTN_WIDTH], qg_ref[...])
    k = norm_rope(z[:, SSM_WIDTH + ATTN_WIDTH:SSM_WIDTH + 2 * ATTN_WIDTH], kg_ref[...])
    k_ref[0] = k
    kb_ref[0] = k.astype(BF16)
    qb_ref[0] = (q * Q_SCALE).astype(BF16)


def _project(h, l_out, tm, g1, w_bf, qg, kg, cos_t, sin_t, gm):
    b, lp, _ = h.shape
    nblk = lp // tm
    tok = lambda w: pl.BlockSpec((1, tm, w), lambda i, j: (i, j, 0))
    full = lambda a: pl.BlockSpec(a.shape, lambda i, j: (0,) * a.ndim)
    return pl.pallas_call(
        _proj_kernel,
        grid=(b, nblk),
        in_specs=[tok(D_MODEL), full(g1), full(w_bf), full(qg), full(kg),
                  pl.BlockSpec((tm, LANES), lambda i, j: (j, 0)),
                  pl.BlockSpec((tm, LANES), lambda i, j: (j, 0)), full(gm)],
        out_specs=[tok(SSM_WIDTH), tok(ATTN_WIDTH), tok(ATTN_WIDTH),
                   tok(ATTN_WIDTH), tok(ATTN_WIDTH), tok(2 * ATTN_WIDTH)],
        out_shape=[jax.ShapeDtypeStruct((b, lp, SSM_WIDTH), F32),
                   jax.ShapeDtypeStruct((b, l_out, ATTN_WIDTH), F32),
                   jax.ShapeDtypeStruct((b, l_out, ATTN_WIDTH), F32),
                   jax.ShapeDtypeStruct((b, lp, ATTN_WIDTH), BF16),
                   jax.ShapeDtypeStruct((b, lp, ATTN_WIDTH), BF16),
                   jax.ShapeDtypeStruct((b, lp, 2 * ATTN_WIDTH), BF16)],
        compiler_params=_cparams(("parallel", "parallel")),
        name="project",
    )(h, g1, w_bf, qg, kg, cos_t, sin_t, gm)


SSM_CHUNK = 512


def _ssm_kernel(sequential, state_row, state_blk, *refs):
    if sequential:
        (u_ref, bbd_ref, cbd_ref, kc_ref, pc_ref, d_ref, wg_ref, bg_ref,
         y_ref, st_ref, x_sc, carry_sc) = refs
        h0_ref = None
    else:
        (u_ref, bbd_ref, cbd_ref, kc_ref, pc_ref, d_ref, wg_ref, bg_ref, h0_ref,
         y_ref, st_ref, x_sc) = refs
        carry_sc = None
    j = pl.program_id(1)
    u = u_ref[0]
    tm = u.shape[0]
    x_sc[...] = jnp.dot(u.astype(BF16), bbd_ref[...], preferred_element_type=F32)

    if sequential:
        @pl.when(j == 0)
        def _():
            carry_sc[...] = jnp.zeros_like(carry_sc)

    def tile_body(i, carry):
        r0 = pl.multiple_of(i * SUBLANES, SUBLANES)
        for c in range(0, SSM_FLAT, SSM_CHUNK):
            re = slice(c, c + SSM_CHUNK)
            im = slice(SSM_FLAT + c, SSM_FLAT + c + SSM_CHUNK)
            xr = x_sc[pl.ds(r0, SUBLANES), re]
            xi = x_sc[pl.ds(r0, SUBLANES), im]
            for di, d in enumerate((1, 2, 4)):
                cr = kc_ref[di, 0, :, re]
                ci = kc_ref[di, 1, :, re]
                rr = pltpu.roll(xr, d, 0)
                ri = pltpu.roll(xi, d, 0)
                xr, xi = xr + cr * rr - ci * ri, xi + cr * ri + ci * rr
            if sequential:
                car_r = carry_sc[:, re]
                car_i = carry_sc[:, im]
            else:
                car_r = h0_ref[0, pl.ds(r0, SUBLANES), re]
                car_i = h0_ref[0, pl.ds(r0, SUBLANES), im]
            pr = pc_ref[0, :, re]
            pi = pc_ref[1, :, re]
            xr, xi = xr + pr * car_r - pi * car_i, xi + pr * car_i + pi * car_r
            x_sc[pl.ds(r0, SUBLANES), re] = xr
            x_sc[pl.ds(r0, SUBLANES), im] = xi
            if sequential:
                carry_sc[:, re] = jnp.broadcast_to(xr[SUBLANES - 1:, :], (SUBLANES, SSM_CHUNK))
                carry_sc[:, im] = jnp.broadcast_to(xi[SUBLANES - 1:, :], (SUBLANES, SSM_CHUNK))
        return carry

    lax.fori_loop(0, tm // SUBLANES, tile_body, 0)

    if sequential:
        @pl.when(j == state_blk)
        def _():
            st_ref[0] = x_sc[state_row:state_row + SUBLANES, :]
    else:
        st_ref[0] = x_sc[...]

    y = jnp.dot(x_sc[...].astype(BF16), cbd_ref[...], preferred_element_type=F32) + d_ref[...] * u
    y = _gelu(y)
    gate = jnp.dot(y.astype(BF16), wg_ref[...], preferred_element_type=F32) + bg_ref[...]
    y_ref[0] = (y * (1.0 / (1.0 + jnp.exp(-gate)))).astype(BF16)


def _ssm(u, tm, last_token, consts, h0=None):
    bbd, cbd, kc, pc, dsk, wg, bg = consts
    b, lp, _ = u.shape
    nblk = lp // tm
    sequential = h0 is None
    full = lambda a: pl.BlockSpec(a.shape, lambda i, j: (0,) * a.ndim)
    in_specs = [pl.BlockSpec((1, tm, SSM_WIDTH), lambda i, j: (i, j, 0)),
                full(bbd), full(cbd), full(kc), full(pc), full(dsk), full(wg), full(bg)]
    args = [u, bbd, cbd, kc, pc, dsk, wg, bg]
    scratch = [pltpu.VMEM((tm, 2 * SSM_FLAT), F32)]
    if sequential:
        state_blk = last_token // tm
        state_row = (last_token % tm) // SUBLANES * SUBLANES
        st_spec = pl.BlockSpec((1, SUBLANES, 2 * SSM_FLAT), lambda i, j: (i, 0, 0))
        st_shape = jax.ShapeDtypeStruct((b, SUBLANES, 2 * SSM_FLAT), F32)
        scratch.append(pltpu.VMEM((SUBLANES, 2 * SSM_FLAT), F32))
        sem = ("parallel", "arbitrary")
    else:
        state_blk = state_row = 0
        in_specs.append(pl.BlockSpec((1, tm, 2 * SSM_FLAT), lambda i, j: (i, j, 0)))
        args.append(jnp.repeat(h0, SUBLANES, axis=1))
        st_spec = pl.BlockSpec((1, tm, 2 * SSM_FLAT), lambda i, j: (i, j, 0))
        st_shape = jax.ShapeDtypeStruct((b, lp, 2 * SSM_FLAT), F32)
        sem = ("parallel", "parallel")
    return pl.pallas_call(
        functools.partial(_ssm_kernel, sequential, state_row, state_blk),
        grid=(b, nblk),
        in_specs=in_specs,
        out_specs=[pl.BlockSpec((1, tm, SSM_WIDTH), lambda i, j: (i, j, 0)), st_spec],
        out_shape=[jax.ShapeDtypeStruct((b, lp, SSM_WIDTH), BF16), st_shape],
        scratch_shapes=scratch,
        compiler_params=_cparams(sem),
        name="s5_prompt" if sequential else "s5_sample",
    )(*args)


def _ssm_consts(a_re, a_im, log_dt, b_re, b_im, c_re, c_im, d_skip, w_glu, b_glu):
    dt = jnp.exp(log_dt.astype(F32))[:, None]
    a_re = a_re.astype(F32)
    a_im = a_im.astype(F32)
    mag = jnp.exp(dt * a_re)
    ab_re = mag * jnp.cos(dt * a_im)
    ab_im = mag * jnp.sin(dt * a_im)
    den = a_re * a_re + a_im * a_im
    z_re = ((ab_re - 1.0) * a_re + ab_im * a_im) / den
    z_im = (ab_im * a_re - (ab_re - 1.0) * a_im) / den
    b_re = b_re.astype(F32)
    b_im = b_im.astype(F32)
    bb_re = z_re[..., None] * b_re - z_im[..., None] * b_im
    bb_im = z_re[..., None] * b_im + z_im[..., None] * b_re
    eye = jnp.eye(SSM_GROUPS, dtype=F32)
    bbd = jnp.concatenate(
        [jnp.einsum('gnc,gh->gchn', bb_re, eye).reshape(SSM_WIDTH, SSM_FLAT),
         jnp.einsum('gnc,gh->gchn', bb_im, eye).reshape(SSM_WIDTH, SSM_FLAT)], axis=1).astype(BF16)
    cbd = jnp.concatenate(
        [jnp.einsum('gcn,gh->gnhc', c_re.astype(F32), eye).reshape(SSM_FLAT, SSM_WIDTH),
         -jnp.einsum('gcn,gh->gnhc', c_im.astype(F32), eye).reshape(SSM_FLAT, SSM_WIDTH)], axis=0).astype(BF16)

    def power(p):
        m = jnp.exp(p * dt * a_re)
        return (m * jnp.cos(p * dt * a_im)).reshape(-1), (m * jnp.sin(p * dt * a_im)).reshape(-1)

    row = jnp.arange(SUBLANES)[:, None]
    kc = []
    for d in (1, 2, 4):
        pr, pi = power(float(d))
        kc.append(jnp.stack([jnp.where(row >= d, pr[None, :], 0.0), jnp.where(row >= d, pi[None, :], 0.0)]))
    kc = jnp.stack(kc)
    rows = [power(float(s + 1)) for s in range(SUBLANES)]
    pc = jnp.stack([jnp.stack([r[0] for r in rows]), jnp.stack([r[1] for r in rows])])
    return (bbd, cbd, kc, pc, d_skip.astype(F32)[None, :], w_glu.astype(BF16), b_glu.astype(F32)[None, :])


def _lambda_value(lamp_ref):
    lp = lamp_ref[...]
    s1 = jnp.sum(lp[0:1] * lp[1:2], axis=1, keepdims=True)
    s2 = jnp.sum(lp[2:3] * lp[3:4], axis=1, keepdims=True)
    return jnp.exp(s1) - jnp.exp(s2) + LAMBDA_INIT


def _stack_sub_queries(q):
    lane = lax.broadcasted_iota(jnp.int32, q.shape, 1)
    zero = jnp.zeros_like(q)
    return jnp.concatenate([jnp.where(lane < ATTN_DK, q, zero), jnp.where(lane >= ATTN_DK, q, zero)], axis=0)


def _flash_update(s, v, m_sc, acc_sc):
    tk = s.shape[1]
    m_prev = m_sc[...]
    m_new = jnp.maximum(m_prev, jnp.max(s, axis=1, keepdims=True))
    p = jnp.exp2(s - pltpu.repeat(m_new, tk // LANES, 1))
    alpha = jnp.exp2(m_prev - m_new)
    acc_sc[...] = pltpu.repeat(alpha, 2, 1) * acc_sc[...] + jnp.dot(p.astype(BF16), v, preferred_element_type=F32)
    m_sc[...] = m_new


def _attn_prompt_kernel(t, lamp_ref, q_ref, k_ref, v_ref, o_ref, q2_sc, m_sc, acc_sc):
    iq = pl.program_id(2)
    q2_sc[...] = _stack_sub_queries(q_ref[0])
    m_sc[...] = jnp.full_like(m_sc, NEG_BIG)
    acc_sc[...] = jnp.zeros_like(acc_sc)

    def scores(ik):
        k0 = pl.multiple_of(ik * t, t)
        s = lax.dot_general(q2_sc[...], k_ref[0, pl.ds(k0, t), :], (((1,), (1,)), ((), ())),
                            preferred_element_type=F32)
        return s, v_ref[0, pl.ds(k0, t), :]

    def causal(s):
        row = lax.broadcasted_iota(jnp.int32, s.shape, 0)
        col = lax.broadcasted_iota(jnp.int32, s.shape, 1)
        return jnp.where(col <= jnp.where(row >= t, row - t, row), s, NEG_BIG)

    def two_tiles(ia, ib, diag_b):
        sa, va = scores(ia)
        sb, vb = scores(ib)
        _flash_update(sa, va, m_sc, acc_sc)
        _flash_update(causal(sb) if diag_b else sb, vb, m_sc, acc_sc)

    def pair_body(i, c):
        two_tiles(2 * i, 2 * i + 1, False)
        return c

    lax.fori_loop(0, iq // 2, pair_body, 0)

    @pl.when(iq % 2 == 1)
    def _():
        two_tiles(iq - 1, iq, True)

    @pl.when(iq % 2 == 0)
    def _():
        s, v = scores(iq)
        _flash_update(causal(s), v, m_sc, acc_sc)

    lam = _lambda_value(lamp_ref)
    acc = acc_sc[...]
    o = acc[:, :ATTN_DV] / acc[:, ATTN_DV:]
    o_ref[0] = o[:t] - lam * o[t:]


def _attn_prompt(lamp, qb, kb, vb, t):
    b, lp, _ = qb.shape
    return pl.pallas_call(
        functools.partial(_attn_prompt_kernel, t),
        grid=(b, N_HEADS, lp // t),
        in_specs=[pl.BlockSpec(lamp.shape, lambda i, h, j: (0, 0)),
                  pl.BlockSpec((1, t, ATTN_DV), lambda i, h, j: (i, j, h)),
                  pl.BlockSpec((1, lp, ATTN_DV), lambda i, h, j: (i, 0, h)),
                  pl.BlockSpec((1, lp, 2 * ATTN_DV), lambda i, h, j: (i, 0, h))],
        out_specs=pl.BlockSpec((1, t, ATTN_DV), lambda i, h, j: (i, j, h)),
        out_shape=jax.ShapeDtypeStruct((b, lp, ATTN_WIDTH), F32),
        scratch_shapes=[pltpu.VMEM((2 * t, ATTN_DV), BF16),
                        pltpu.VMEM((2 * t, LANES), F32),
                        pltpu.VMEM((2 * t, 2 * ATTN_DV), F32)],
        compiler_params=_cparams(("parallel", "parallel", "parallel")),
        name="attn_prompt",
    )(lamp, qb, kb, vb)


def _attn_sample_kernel(n_pages, s_len, pt_ref, lamp_ref, q_ref, kn_ref, vn_ref, *refs):
    kt_refs = refs[:n_pages]
    v_refs = refs[n_pages:2 * n_pages]
    o_ref, q2_sc, m_sc, l_sc, acc_sc = refs[2 * n_pages:]
    j = pl.program_id(1)
    hr = 2 * s_len

    @pl.when(j == 0)
    def _():
        q = q_ref[0]
        lane = lax.broadcasted_iota(jnp.int32, q.shape, 1)
        zero = jnp.zeros_like(q)
        q2_sc[...] = jnp.concatenate(
            [jnp.where((lane >= lo) & (lane < lo + ATTN_DK), q, zero) for lo in range(0, ATTN_WIDTH, ATTN_DK)], axis=0)
        m_sc[...] = jnp.full_like(m_sc, NEG_BIG)
        l_sc[...] = jnp.zeros_like(l_sc)
        acc_sc[...] = jnp.zeros_like(acc_sc)

    def update(s, values):
        tk = s.shape[1]
        m_prev = m_sc[...]
        m_new = jnp.maximum(m_prev, jnp.max(s, axis=1, keepdims=True))
        p = jnp.exp2(s - pltpu.repeat(m_new, tk // LANES, 1))
        alpha = jnp.exp2(m_prev - m_new)
        l_sc[...] = alpha * l_sc[...] + jnp.sum(p, axis=1, keepdims=True)
        pb = p.astype(BF16)
        for h in range(N_HEADS):
            rs = slice(h * hr, (h + 1) * hr)
            acc_sc[rs, :] = alpha[rs] * acc_sc[rs, :] + jnp.dot(pb[rs], values(h), preferred_element_type=F32)
        m_sc[...] = m_new

    kt = jnp.concatenate([r[0] for r in kt_refs], axis=1).astype(BF16)
    s = jnp.dot(q2_sc[...], kt, preferred_element_type=F32)
    update(s, lambda h: jnp.concatenate([r[0, pl.ds(h, LANES, stride=N_HEADS), :] for r in v_refs],
                                        axis=0).astype(BF16))

    @pl.when(j == pl.num_programs(1) - 1)
    def _():
        lam = _lambda_value(lamp_ref)
        pad_k = jnp.zeros((LANES - s_len, ATTN_WIDTH), BF16)
        pad_v = jnp.zeros((LANES - s_len, ATTN_DV), BF16)
        s_own = lax.dot_general(q2_sc[...], jnp.concatenate([kn_ref[0], pad_k], axis=0),
                                (((1,), (1,)), ((), ())), preferred_element_type=F32)
        row = lax.broadcasted_iota(jnp.int32, s_own.shape, 0)
        col = lax.broadcasted_iota(jnp.int32, s_own.shape, 1)
        s_own = jnp.where(col <= (row & (s_len - 1)), s_own, NEG_BIG)
        update(s_own, lambda h: jnp.concatenate([vn_ref[0, :, 2 * h * ATTN_DV:(2 * h + 1) * ATTN_DV], pad_v], axis=0))
        o = acc_sc[...] / l_sc[...]
        for h in range(N_HEADS):
            o_ref[0, :, h * ATTN_DV:(h + 1) * ATTN_DV] = (o[h * hr:h * hr + s_len]
                                                          - lam * o[h * hr + s_len:(h + 1) * hr])


def _attn_sample(lamp, qb, kb, vb, cache_kt, cache_v, page_table, n_pages):
    db, s_len, _ = qb.shape
    n_past_pages = page_table.shape[1]
    assert cache_kt.shape[2] == LANES and s_len & (s_len - 1) == 0 and n_past_pages % n_pages == 0
    steps = n_past_pages // n_pages
    tok = lambda w: pl.BlockSpec((1, s_len, w), lambda i, j, pt: (i, 0, 0))

    def page_spec(p):
        return pl.BlockSpec((1, ATTN_WIDTH, LANES), lambda i, j, pt: (pt[i, j * n_pages + p], 0, 0))

    rows = 2 * s_len * N_HEADS
    return pl.pallas_call(
        functools.partial(_attn_sample_kernel, n_pages, s_len),
        grid_spec=pltpu.PrefetchScalarGridSpec(
            num_scalar_prefetch=1,
            grid=(db, steps),
            in_specs=[pl.BlockSpec(lamp.shape, lambda i, j, pt: (0, 0)),
                      tok(ATTN_WIDTH), tok(ATTN_WIDTH), tok(2 * ATTN_WIDTH)]
                     + [page_spec(p) for p in range(n_pages)] * 2,
            out_specs=tok(ATTN_WIDTH),
            scratch_shapes=[pltpu.VMEM((rows, ATTN_WIDTH), BF16),
                            pltpu.VMEM((rows, LANES), F32),
                            pltpu.VMEM((rows, LANES), F32),
                            pltpu.VMEM((rows, ATTN_DV), F32)]),
        out_shape=jax.ShapeDtypeStruct((db, s_len, ATTN_WIDTH), F32),
        compiler_params=_cparams(("parallel", "arbitrary")),
        name="attn_sample",
    )(page_table, lamp, qb, kb, vb, *([cache_kt] * n_pages), *([cache_v] * n_pages))


def _finish_kernel(h_ref, ys_ref, o_ref, sg_ref, wout_ref, g2_ref, wq_ref, keys_ref,
                   h1_ref, n2_ref, st_ref):
    o = o_ref[0]
    parts = []
    for h in range(N_HEADS):
        oh = o[:, h * ATTN_DV:(h + 1) * ATTN_DV]
        parts.append(oh * lax.rsqrt(jnp.mean(oh * oh, axis=-1, keepdims=True) + EPS))
    on = jnp.concatenate(parts, axis=1) * sg_ref[...] * (1.0 - LAMBDA_INIT)
    mix = jnp.concatenate([ys_ref[0], on.astype(BF16)], axis=1)
    h1 = h_ref[0] + jnp.dot(mix, wout_ref[...], preferred_element_type=F32)
    h1_ref[0] = h1
    n2 = (h1 * lax.rsqrt(jnp.mean(h1 * h1, axis=-1, keepdims=True) + EPS) * g2_ref[...]).astype(BF16)
    n2_ref[0] = n2
    qt = lax.dot_general(wq_ref[...], n2, (((1,), (1,)), ((), ())), preferred_element_type=F32)
    for hp in range(2 * PEER_HEADS):
        st_ref[0, hp] = jnp.dot(keys_ref[hp], qt[hp * N_KEYS:(hp + 1) * N_KEYS].astype(BF16),
                                preferred_element_type=F32)


def _finish(h, ys, o, tm, sg, wout, g2, wq_t, keys):
    b, lp, _ = h.shape
    nblk = lp // tm
    tok = lambda w: pl.BlockSpec((1, tm, w), lambda i, j: (i, j, 0))
    full = lambda a: pl.BlockSpec(a.shape, lambda i, j: (0,) * a.ndim)
    return pl.pallas_call(
        _finish_kernel,
        grid=(b, nblk),
        in_specs=[tok(D_MODEL), tok(SSM_WIDTH), tok(ATTN_WIDTH), full(sg), full(wout), full(g2),
                  full(wq_t), full(keys)],
        out_specs=[tok(D_MODEL), tok(D_MODEL),
                   pl.BlockSpec((1, 2 * PEER_HEADS, N_KEYS, tm), lambda i, j: (i, 0, 0, j))],
        out_shape=[jax.ShapeDtypeStruct((b, lp, D_MODEL), F32),
                   jax.ShapeDtypeStruct((b, lp, D_MODEL), BF16),
                   jax.ShapeDtypeStruct((b, 2 * PEER_HEADS, N_KEYS, lp), F32)],
        compiler_params=_cparams(("parallel", "parallel")),
        name="finish",
    )(h, ys, o, sg, wout, g2, wq_t, keys)


def _cmpx(v, i, j):
    hi = jnp.maximum(v[i], v[j])
    lo = jnp.minimum(v[i], v[j])
    v[i], v[j] = hi, lo


def _bitonic_merge_desc(v):
    n = len(v)
    v = list(v)
    d = n // 2
    while d >= 1:
        for i in range(n):
            if (i % (2 * d)) < d:
                _cmpx(v, i, i + d)
        d //= 2
    return v


def _sort_desc(v):
    n = len(v)
    if n == 1:
        return list(v)
    a = _sort_desc(v[:n // 2])
    b = _sort_desc(v[n // 2:])
    return _bitonic_merge_desc(a + b[::-1])


def _merge_top(t, s):
    n = len(t)
    m = list(t)
    for r in range(n):
        q = n - 1 - r
        if q < len(s):
            m[r] = jnp.maximum(t[r], s[q])
    return _bitonic_merge_desc(m)


def _peer_select_kernel(s_ref, tau_ref, w1_ref, e2_ref):
    tb = s_ref.shape[-1]
    sub = lax.broadcasted_iota(jnp.int32, (SUBLANES, LANES), 0)
    k = PEER_TOPK

    def lane_group(ln):
        tops = []
        for hp in range(2 * PEER_HEADS):
            col = _sort_desc([s_ref[0, hp, r * SUBLANES:(r + 1) * SUBLANES, ln] for r in range(N_KEYS // SUBLANES)])
            for shift in (4, 2, 1):
                other = [pltpu.roll(c, shift, 0) for c in col]
                col = _merge_top(col, other)
            tops.append(col)

        def pack(lists):
            out = []
            for r in range(k):
                x = lists[0][r]
                for h in range(1, PEER_HEADS):
                    x = jnp.where(sub == h, lists[h][r], x)
                out.append(x)
            return out

        a = pack([tops[2 * h] for h in range(PEER_HEADS)])
        b = pack([tops[2 * h + 1] for h in range(PEER_HEADS)])
        t = [a[0] + b[q] for q in range(k)]
        for i in range(1, k // 2):
            t = _merge_top(t, [a[i] + b[q] for q in range(k // (i + 1))])
        t = _merge_top(t, [a[i] + b[0] for i in range(k // 2, k)])
        theta = t[k - 1]
        z = jnp.ones_like(theta)
        for r in range(1, k):
            z = z + jnp.exp(t[r] - t[0])
        inv_z = 1.0 / z
        taus = []
        for r in range(k):
            tau = jnp.full_like(theta, POS_BIG)
            for q in range(k // (r + 1)):
                tau = jnp.where(a[r] + b[q] >= theta, b[q], tau)
            taus.append(tau)

        for h in range(PEER_HEADS):
            bc = lambda x: jnp.broadcast_to(x[h:h + 1, :], (SUBLANES, LANES))
            a_h = tops[2 * h]
            tau_h = [bc(x) for x in taus]
            inv_z_h = bc(inv_z)
            a0 = a_h[0]
            b0 = tops[2 * h + 1][0]
            for r in range(N_KEYS // SUBLANES):
                rs = slice(r * SUBLANES, (r + 1) * SUBLANES)
                s1 = s_ref[0, 2 * h, rs, ln]
                tau = jnp.full_like(s1, POS_BIG)
                for q in range(k):
                    tau = jnp.where(s1 == a_h[q], tau_h[q], tau)
                tau_ref[0, h, rs, ln] = tau
                w1_ref[0, h, rs, ln] = jnp.exp(s1 - a0) * inv_z_h
                e2_ref[0, h, rs, ln] = jnp.exp(s_ref[0, 2 * h + 1, rs, ln] - b0)

    for g in range(tb // LANES):
        lane_group(slice(g * LANES, (g + 1) * LANES))


def _peer_select(st):
    tb = LANES
    b, _, _, lp = st.shape
    out = jax.ShapeDtypeStruct((b, PEER_HEADS, N_KEYS, lp), F32)
    spec = pl.BlockSpec((1, PEER_HEADS, N_KEYS, tb), lambda i, j: (i, 0, 0, j))
    return pl.pallas_call(
        _peer_select_kernel,
        grid=(b, lp // tb),
        in_specs=[pl.BlockSpec((1, 2 * PEER_HEADS, N_KEYS, tb), lambda i, j: (i, 0, 0, j))],
        out_specs=[spec, spec, spec],
        out_shape=[out, out, out],
        compiler_params=_cparams(("parallel", "parallel")),
        name="peer_select",
    )(st)


def _peer_dense_kernel(ec, n2_ref, u_ref, vt_ref, s2_ref, e2_ref, tau_ref, w1_ref, h1_ref,
                       out_ref, a_sc, acc_sc):
    c = pl.program_id(2)
    tb = n2_ref.shape[1]

    @pl.when(c == 0)
    def _():
        acc_sc[...] = jnp.zeros_like(acc_sc)

    n2 = n2_ref[0]
    n_sub = ec // PEER_SUB

    def activations(sb):
        rows = slice(sb * PEER_SUB, (sb + 1) * PEER_SUB)
        a_sc[rows, :] = lax.dot_general(u_ref[rows, :], n2, (((1,), (1,)), ((), ())), preferred_element_type=F32)

    def gate(sb):
        il0 = sb * (PEER_SUB // N_KEYS)
        for cc in range(tb // LANES):
            ln = slice(cc * LANES, (cc + 1) * LANES)
            bc = lambda ref, h, il: jnp.broadcast_to(ref[0, h, il:il + 1, ln], (SUBLANES, LANES))
            tau = [[bc(tau_ref, h, il0 + d) for h in range(PEER_HEADS)] for d in range(2)]
            w1 = [[bc(w1_ref, h, il0 + d) for h in range(PEER_HEADS)] for d in range(2)]
            for jv in range(N_KEYS // SUBLANES):
                rs = slice(jv * SUBLANES, (jv + 1) * SUBLANES)
                g = [jnp.zeros((SUBLANES, LANES), F32) for _ in range(2)]
                for h in range(PEER_HEADS):
                    s2 = s2_ref[0, h, 0, rs, ln]
                    e2 = e2_ref[0, h, rs, ln]
                    for d in range(2):
                        g[d] = g[d] + jnp.where(s2 >= tau[d][h], e2 * w1[d][h], 0.0)
                for d in range(2):
                    r0 = (il0 + d) * N_KEYS + jv * SUBLANES
                    a_sc[r0:r0 + SUBLANES, ln] = g[d] * _gelu(a_sc[r0:r0 + SUBLANES, ln])

    part = None
    activations(0)
    for sb in range(n_sub):
        if sb + 1 < n_sub:
            activations(sb + 1)
        gate(sb)
        rows = slice(sb * PEER_SUB, (sb + 1) * PEER_SUB)
        d = jnp.dot(vt_ref[:, rows], a_sc[rows, :].astype(BF16), preferred_element_type=F32)
        part = d if part is None else part + d
    acc_sc[...] += part

    @pl.when(c == pl.num_programs(2) - 1)
    def _():
        out_ref[0] = h1_ref[0] + acc_sc[...].T


def _peer_dense(n2, u_bf, vt_bf, st, e2, tau, w1, h1, tb, ec):
    b, lp, _ = n2.shape
    rows = ec // N_KEYS
    st5 = st.reshape(b, PEER_HEADS, 2, N_KEYS, lp)
    return pl.pallas_call(
        functools.partial(_peer_dense_kernel, ec),
        grid=(b, lp // tb, N_EXPERTS // ec),
        in_specs=[pl.BlockSpec((1, tb, D_MODEL), lambda i, j, c: (i, j, 0)),
                  pl.BlockSpec((ec, D_MODEL), lambda i, j, c: (c, 0)),
                  pl.BlockSpec((D_MODEL, ec), lambda i, j, c: (0, c)),
                  pl.BlockSpec((1, PEER_HEADS, 1, N_KEYS, tb), lambda i, j, c: (i, 0, 1, 0, j)),
                  pl.BlockSpec((1, PEER_HEADS, N_KEYS, tb), lambda i, j, c: (i, 0, 0, j)),
                  pl.BlockSpec((1, PEER_HEADS, rows, tb), lambda i, j, c: (i, 0, c, j)),
                  pl.BlockSpec((1, PEER_HEADS, rows, tb), lambda i, j, c: (i, 0, c, j)),
                  pl.BlockSpec((1, tb, D_MODEL), lambda i, j, c: (i, j, 0))],
        out_specs=pl.BlockSpec((1, tb, D_MODEL), lambda i, j, c: (i, j, 0)),
        out_shape=jax.ShapeDtypeStruct((b, lp, D_MODEL), F32),
        scratch_shapes=[pltpu.VMEM((ec, tb), F32), pltpu.VMEM((D_MODEL, tb), F32)],
        compiler_params=_cparams(("parallel", "parallel", "arbitrary")),
        name="peer_dense",
    )(n2, u_bf, vt_bf, st5, e2, tau, w1, h1)


def _rope_tables(pos):
    half = ATTN_DK // 2
    inv_freq = ROPE_THETA ** (-jnp.arange(half, dtype=F32) * 2.0 / ATTN_DK)
    ang = pos[:, None] * inv_freq[None, :]
    cos = jnp.cos(ang)
    sin = jnp.sin(ang)
    return (jnp.concatenate([cos, cos, cos, cos], axis=1),
            jnp.concatenate([-sin, sin, -sin, sin], axis=1))


def _round_up(x, m):
    return (x + m - 1) // m * m


def _layer(h, l_real, pos, weights, tm, tb, ec, attend):
    (g1, w_in, qg, kg, gm, ssm_c, sg, wout, g2, wq_t, keys, u_bf, vt_bf) = weights
    cos_t, sin_t = _rope_tables(pos)
    u, k, v, qb, kb, vb = _project(h, l_real, tm, g1, w_in, qg, kg, cos_t, sin_t, gm)
    o, ys, st = attend(u, qb, kb, vb, ssm_c)
    h1, n2, sc = _finish(h, ys, o, tb, sg, wout, g2, wq_t, keys)
    tau, w1, e2 = _peer_select(sc)
    out = _peer_dense(n2, u_bf, vt_bf, sc, e2, tau, w1, h1, tb, ec)
    return out, k, v, st


def kernel(x_prompt, x_sample, cache_k, cache_v, state_ssm_re, state_ssm_im, page_table, meta_tokens, norm1_g, w_in, q_norm_g, k_norm_g, lambda_q1, lambda_k1, lambda_q2, lambda_k2, subln_g, ssm_A_re, ssm_A_im, ssm_log_dt, ssm_B_re, ssm_B_im, ssm_C_re, ssm_C_im, ssm_D, w_glu, b_glu, w_out, norm2_g, peer_w_q, peer_keys, peer_u, peer_v):
    assert w_in.shape[0] == 1, "single-layer trunk"
    bp, seq, _ = x_prompt.shape
    db, ds, _ = x_sample.shape
    lp_real = seq + N_META
    assert lp_real % SUBLANES == 0 and ds == SUBLANES
    tm_p, t_attn, tb, ec = 768, 512, 512, 1024
    lp = _round_up(lp_real, math.lcm(tm_p, t_attn, tb))
    past = page_table.shape[1] * cache_k.shape[2]

    gm = jnp.kron(jnp.eye(ATTN_WIDTH // ATTN_DK, dtype=F32),
                  jnp.full((ATTN_DK, ATTN_DK), 1.0 / ATTN_DK, F32)).astype(BF16)
    lamp = jnp.stack([lambda_q1[0], lambda_k1[0], lambda_q2[0], lambda_k2[0]]).astype(F32)
    ssm_c = _ssm_consts(ssm_A_re[0], ssm_A_im[0], ssm_log_dt[0], ssm_B_re[0], ssm_B_im[0],
                        ssm_C_re[0], ssm_C_im[0], ssm_D[0], w_glu[0], b_glu[0])
    weights = (norm1_g[0].astype(F32)[None, :], w_in[0].astype(BF16),
               jnp.tile(q_norm_g[0].astype(F32), ATTN_WIDTH // ATTN_DK)[None, :],
               jnp.tile(k_norm_g[0].astype(F32), ATTN_WIDTH // ATTN_DK)[None, :],
               gm, ssm_c,
               jnp.tile(subln_g[0].astype(F32), N_HEADS)[None, :], w_out[0].astype(BF16),
               norm2_g[0].astype(F32)[None, :], peer_w_q[0].T.astype(BF16),
               peer_keys[0].reshape(2 * PEER_HEADS, N_KEYS, N_KEYS).astype(BF16),
               peer_u[0].astype(BF16), peer_v[0].T.astype(BF16))

    hp = jnp.concatenate([jnp.broadcast_to(meta_tokens.astype(F32)[None], (bp, N_META, D_MODEL)),
                          x_prompt.astype(F32),
                          jnp.zeros((bp, lp - lp_real, D_MODEL), F32)], axis=1)

    def attend_prompt(u, qb, kb, vb, consts):
        ys, st = _ssm(u, tm_p, lp_real - 1, consts)
        return _attn_prompt(lamp, qb, kb, vb, t_attn), ys, st

    hp_out, k_p, v_p, st_p = _layer(hp, lp_real, jnp.arange(lp, dtype=F32), weights, tm_p, tb, ec, attend_prompt)

    n_tok = db * ds
    hs = x_sample.astype(F32).reshape(1, n_tok, D_MODEL)
    h0 = jnp.concatenate([state_ssm_re[0].reshape(db, SSM_FLAT), state_ssm_im[0].reshape(db, SSM_FLAT)],
                         axis=1).astype(F32)[None]
    n_pool, page = cache_k.shape[1], cache_k.shape[2]
    ckt = jnp.transpose(cache_k[0], (0, 2, 3, 4, 1)).reshape(n_pool, ATTN_WIDTH, page)
    cv = cache_v[0].reshape(n_pool, page * N_HEADS, ATTN_DV)

    def attend_sample(u, qb, kb, vb, consts):
        ys, st = _ssm(u, 256, None, consts, h0=h0)
        seqs = lambda a: a.reshape(db, ds, a.shape[-1])
        o = _attn_sample(lamp, seqs(qb), seqs(kb), seqs(vb), ckt, cv, page_table, 16)
        return o.reshape(1, n_tok, ATTN_WIDTH), ys, st

    pos_s = jnp.tile(past + jnp.arange(ds, dtype=F32), db)
    hs_out, k_s, v_s, st_s = _layer(hs, n_tok, pos_s, weights, 512, tb, ec, attend_sample)

    y_prompt = hp_out[:, N_META:lp_real].astype(x_prompt.dtype)
    y_sample = hs_out.reshape(db, ds, D_MODEL).astype(x_sample.dtype)
    row = (lp_real - 1) % SUBLANES
    kd, vd, sd = cache_k.dtype, cache_v.dtype, state_ssm_re.dtype
    return (y_prompt, y_sample,
            k_p.reshape(1, bp, lp_real, N_HEADS, 2, ATTN_DK).astype(kd),
            v_p.reshape(1, bp, lp_real, N_HEADS, ATTN_DV).astype(vd),
            st_p[:, row, :SSM_FLAT].reshape(1, bp, SSM_GROUPS, SSM_STATE).astype(sd),
            st_p[:, row, SSM_FLAT:].reshape(1, bp, SSM_GROUPS, SSM_STATE).astype(state_ssm_im.dtype),
            k_s.reshape(1, db, ds, N_HEADS, 2, ATTN_DK).astype(kd),
            v_s.reshape(1, db, ds, N_HEADS, ATTN_DV).astype(vd),
            st_s[0, ds - 1::ds, :SSM_FLAT].reshape(1, db, SSM_GROUPS, SSM_STATE).astype(sd),
            st_s[0, ds - 1::ds, SSM_FLAT:].reshape(1, db, SSM_GROUPS, SSM_STATE).astype(state_ssm_im.dtype))
```

```python
import functools
import math

import jax
import jax.numpy as jnp
from jax import lax
from jax.experimental import pallas as pl
from jax.experimental.pallas import tpu as pltpu

F32 = jnp.float32
BF16 = jnp.bfloat16

D_MODEL = 1024
N_META = 16
SSM_WIDTH = 512
SSM_GROUP = 16
SSM_GROUPS = 32
SSM_STATE = 64
SSM_FLAT = SSM_GROUPS * SSM_STATE
ATTN_WIDTH = 512
ATTN_DV = 128
N_HEADS = 4
ATTN_DK = 64
IN_WIDTH = 2048
ROPE_THETA = 10000.0
N_KEYS = 128
N_EXPERTS = N_KEYS * N_KEYS
PEER_HEADS = 8
PEER_TOPK = 16
PEER_SUB = 2 * N_KEYS
EPS = 1e-6
NEG_BIG = -1e30
POS_BIG = 3e38
LAMBDA_INIT = 0.8 - 0.6 * math.exp(-0.3 * 0)
Q_SCALE = ATTN_DK ** -0.5 * math.log2(math.e)

SUBLANES = 8
LANES = 128
BF16_ROWS = 2 * SUBLANES
VMEM_LIMIT = 56 * 1024 * 1024


def _cparams(sem):
    return pltpu.CompilerParams(dimension_semantics=sem, vmem_limit_bytes=VMEM_LIMIT)


GELU_C1 = 0.7978845608028654
GELU_C2 = GELU_C1 * 0.044715


def _gelu(x):
    return x * (0.5 + 0.5 * jnp.tanh(x * (GELU_C1 + GELU_C2 * (x * x))))


def _proj_kernel(x_ref, g1_ref, w_ref, qg_ref, kg_ref, cos_ref, sin_ref, gm_ref,
                 u_ref, k_ref, v_ref, qb_ref, kb_ref, vb_ref):
    x = x_ref[0]
    tm = x.shape[0]
    ms = jnp.mean(x * x, axis=-1, keepdims=True)
    n = (x * lax.rsqrt(ms + EPS) * g1_ref[...]).astype(BF16)
    z = jnp.dot(n, w_ref[...], preferred_element_type=F32)
    u_ref[0] = z[:, :SSM_WIDTH]
    v = z[:, SSM_WIDTH + 2 * ATTN_WIDTH:]
    v_ref[0] = v
    vb = v.astype(BF16)
    ones = jnp.ones((tm, ATTN_DV), BF16)
    vb_ref[0] = jnp.concatenate([t for h in range(N_HEADS) for t in (vb[:, h * ATTN_DV:(h + 1) * ATTN_DV], ones)], axis=1)

    cos = jnp.concatenate([cos_ref[...]] * 4, axis=1)
    sin = jnp.concatenate([sin_ref[...]] * 4, axis=1)
    lane = lax.broadcasted_iota(jnp.int32, (tm, ATTN_WIDTH), 1)
    first_half = (lane & (ATTN_DK - 1)) < (ATTN_DK // 2)
    gm = gm_ref[...]

    def norm_rope(t, g):
        sq = t * t
        hi = sq.astype(BF16)
        lo = (sq - hi.astype(F32)).astype(BF16)
        ms64 = (jnp.dot(hi, gm, preferred_element_type=F32)
                + jnp.dot(lo, gm, preferred_element_type=F32))
        tn = t * lax.rsqrt(ms64 + EPS) * g
        swapped = jnp.where(first_half,
                            pltpu.roll(tn, ATTN_WIDTH - ATTN_DK // 2, 1),
                            pltpu.roll(tn, ATTN_DK // 2, 1))
        return tn * cos + swapped * sin

    q = norm_rope(z[:, SSM_WIDTH:SSM_WIDTH + ATTN_WIDTH], qg_ref[...])
    k = norm_rope(z[:, SSM_WIDTH + ATTN_WIDTH:SSM_WIDTH + 2 * ATTN_WIDTH], kg_ref[...])
    k_ref[0] = k
    kb_ref[0] = k.astype(BF16)
    qb_ref[0] = (q * Q_SCALE).astype(BF16)


def _project(h, l_out, tm, g1, w_bf, qg, kg, cos_t, sin_t, gm):
    b, lp, _ = h.shape
    nblk = lp // tm
    tok = lambda w: pl.BlockSpec((1, tm, w), lambda i, j: (i, j, 0))
    full = lambda a: pl.BlockSpec(a.shape, lambda i, j: (0,) * a.ndim)
    return pl.pallas_call(
        _proj_kernel,
        grid=(b, nblk),
        in_specs=[tok(D_MODEL), full(g1), full(w_bf), full(qg), full(kg),
                  pl.BlockSpec((tm, LANES), lambda i, j: (j, 0)),
                  pl.BlockSpec((tm, LANES), lambda i, j: (j, 0)), full(gm)],
        out_specs=[tok(SSM_WIDTH), tok(ATTN_WIDTH), tok(ATTN_WIDTH),
                   tok(ATTN_WIDTH), tok(ATTN_WIDTH), tok(2 * ATTN_WIDTH)],
        out_shape=[jax.ShapeDtypeStruct((b, lp, SSM_WIDTH), F32),
                   jax.ShapeDtypeStruct((b, l_out, ATTN_WIDTH), F32),
                   jax.ShapeDtypeStruct((b, l_out, ATTN_WIDTH), F32),
                   jax.ShapeDtypeStruct((b, lp, ATTN_WIDTH), BF16),
                   jax.ShapeDtypeStruct((b, lp, ATTN_WIDTH), BF16),
                   jax.ShapeDtypeStruct((b, lp, 2 * ATTN_WIDTH), BF16)],
        compiler_params=_cparams(("parallel", "parallel")),
        name="project",
    )(h, g1, w_bf, qg, kg, cos_t, sin_t, gm)


SSM_CHUNK = 512


def _ssm_kernel(sequential, state_row, state_blk, *refs):
    if sequential:
        (u_ref, bbd_ref, cbd_ref, kc_ref, pc_ref, d_ref, wg_ref, bg_ref,
         y_ref, st_ref, x_sc, carry_sc) = refs
        h0_ref = None
    else:
        (u_ref, bbd_ref, cbd_ref, kc_ref, pc_ref, d_ref, wg_ref, bg_ref, h0_ref,
         y_ref, st_ref, x_sc) = refs
        carry_sc = None
    j = pl.program_id(1)
    u = u_ref[0]
    tm = u.shape[0]
    x_sc[...] = jnp.dot(u.astype(BF16), bbd_ref[...], preferred_element_type=F32)

    if sequential:
        @pl.when(j == 0)
        def _():
            carry_sc[...] = jnp.zeros_like(carry_sc)

    def tile_body(i, carry):
        r0 = pl.multiple_of(i * SUBLANES, SUBLANES)
        for c in range(0, SSM_FLAT, SSM_CHUNK):
            re = slice(c, c + SSM_CHUNK)
            im = slice(SSM_FLAT + c, SSM_FLAT + c + SSM_CHUNK)
            xr = x_sc[pl.ds(r0, SUBLANES), re]
            xi = x_sc[pl.ds(r0, SUBLANES), im]
            for di, d in enumerate((1, 2, 4)):
                cr = kc_ref[di, 0, :, re]
                ci = kc_ref[di, 1, :, re]
                rr = pltpu.roll(xr, d, 0)
                ri = pltpu.roll(xi, d, 0)
                xr, xi = xr + cr * rr - ci * ri, xi + cr * ri + ci * rr
            if sequential:
                car_r = carry_sc[:, re]
                car_i = carry_sc[:, im]
            else:
                car_r = h0_ref[0, pl.ds(r0, SUBLANES), re]
                car_i = h0_ref[0, pl.ds(r0, SUBLANES), im]
            pr = pc_ref[0, :, re]
            pi = pc_ref[1, :, re]
            xr, xi = xr + pr * car_r - pi * car_i, xi + pr * car_i + pi * car_r
            x_sc[pl.ds(r0, SUBLANES), re] = xr
            x_sc[pl.ds(r0, SUBLANES), im] = xi
            if sequential:
                carry_sc[:, re] = jnp.broadcast_to(xr[SUBLANES - 1:, :], (SUBLANES, SSM_CHUNK))
                carry_sc[:, im] = jnp.broadcast_to(xi[SUBLANES - 1:, :], (SUBLANES, SSM_CHUNK))
        return carry

    lax.fori_loop(0, tm // SUBLANES, tile_body, 0)

    if sequential:
        @pl.when(j == state_blk)
        def _():
            st_ref[0] = x_sc[state_row:state_row + SUBLANES, :]
    else:
        st_ref[0] = x_sc[...]

    y = jnp.dot(x_sc[...].astype(BF16), cbd_ref[...], preferred_element_type=F32) + d_ref[...] * u
    y = _gelu(y)
    gate = jnp.dot(y.astype(BF16), wg_ref[...], preferred_element_type=F32) + bg_ref[...]
    y_ref[0] = (y * (1.0 / (1.0 + jnp.exp(-gate)))).astype(BF16)


def _ssm(u, tm, last_token, consts, h0=None):
    bbd, cbd, kc, pc, dsk, wg, bg = consts
    b, lp, _ = u.shape
    nblk = lp // tm
    sequential = h0 is None
    full = lambda a: pl.BlockSpec(a.shape, lambda i, j: (0,) * a.ndim)
    in_specs = [pl.BlockSpec((1, tm, SSM_WIDTH), lambda i, j: (i, j, 0)),
                full(bbd), full(cbd), full(kc), full(pc), full(dsk), full(wg), full(bg)]
    args = [u, bbd, cbd, kc, pc, dsk, wg, bg]
    scratch = [pltpu.VMEM((tm, 2 * SSM_FLAT), F32)]
    if sequential:
        state_blk = last_token // tm
        state_row = (last_token % tm) // SUBLANES * SUBLANES
        st_spec = pl.BlockSpec((1, SUBLANES, 2 * SSM_FLAT), lambda i, j: (i, 0, 0))
        st_shape = jax.ShapeDtypeStruct((b, SUBLANES, 2 * SSM_FLAT), F32)
        scratch.append(pltpu.VMEM((SUBLANES, 2 * SSM_FLAT), F32))
        sem = ("parallel", "arbitrary")
    else:
        state_blk = state_row = 0
        in_specs.append(pl.BlockSpec((1, tm, 2 * SSM_FLAT), lambda i, j: (i, j, 0)))
        args.append(jnp.repeat(h0, SUBLANES, axis=1))
        st_spec = pl.BlockSpec((1, tm, 2 * SSM_FLAT), lambda i, j: (i, j, 0))
        st_shape = jax.ShapeDtypeStruct((b, lp, 2 * SSM_FLAT), F32)
        sem = ("parallel", "parallel")
    return pl.pallas_call(
        functools.partial(_ssm_kernel, sequential, state_row, state_blk),
        grid=(b, nblk),
        in_specs=in_specs,
        out_specs=[pl.BlockSpec((1, tm, SSM_WIDTH), lambda i, j: (i, j, 0)), st_spec],
        out_shape=[jax.ShapeDtypeStruct((b, lp, SSM_WIDTH), BF16), st_shape],
        scratch_shapes=scratch,
        compiler_params=_cparams(sem),
        name="s5_prompt" if sequential else "s5_sample",
    )(*args)


def _ssm_consts(a_re, a_im, log_dt, b_re, b_im, c_re, c_im, d_skip, w_glu, b_glu):
    dt = jnp.exp(log_dt.astype(F32))[:, None]
    a_re = a_re.astype(F32)
    a_im = a_im.astype(F32)
    mag = jnp.exp(dt * a_re)
    ab_re = mag * jnp.cos(dt * a_im)
    ab_im = mag * jnp.sin(dt * a_im)
    den = a_re * a_re + a_im * a_im
    z_re = ((ab_re - 1.0) * a_re + ab_im * a_im) / den
    z_im = (ab_im * a_re - (ab_re - 1.0) * a_im) / den
    b_re = b_re.astype(F32)
    b_im = b_im.astype(F32)
    bb_re = z_re[..., None] * b_re - z_im[..., None] * b_im
    bb_im = z_re[..., None] * b_im + z_im[..., None] * b_re
    eye = jnp.eye(SSM_GROUPS, dtype=F32)
    bbd = jnp.concatenate(
        [jnp.einsum('gnc,gh->gchn', bb_re, eye).reshape(SSM_WIDTH, SSM_FLAT),
         jnp.einsum('gnc,gh->gchn', bb_im, eye).reshape(SSM_WIDTH, SSM_FLAT)], axis=1).astype(BF16)
    cbd = jnp.concatenate(
        [jnp.einsum('gcn,gh->gnhc', c_re.astype(F32), eye).reshape(SSM_FLAT, SSM_WIDTH),
         -jnp.einsum('gcn,gh->gnhc', c_im.astype(F32), eye).reshape(SSM_FLAT, SSM_WIDTH)], axis=0).astype(BF16)

    def power(p):
        m = jnp.exp(p * dt * a_re)
        return (m * jnp.cos(p * dt * a_im)).reshape(-1), (m * jnp.sin(p * dt * a_im)).reshape(-1)

    row = jnp.arange(SUBLANES)[:, None]
    kc = []
    for d in (1, 2, 4):
        pr, pi = power(float(d))
        kc.append(jnp.stack([jnp.where(row >= d, pr[None, :], 0.0), jnp.where(row >= d, pi[None, :], 0.0)]))
    kc = jnp.stack(kc)
    rows = [power(float(s + 1)) for s in range(SUBLANES)]
    pc = jnp.stack([jnp.stack([r[0] for r in rows]), jnp.stack([r[1] for r in rows])])
    return (bbd, cbd, kc, pc, d_skip.astype(F32)[None, :], w_glu.astype(BF16), b_glu.astype(F32)[None, :])


def _lambda_value(lamp_ref):
    lp = lamp_ref[...]
    s1 = jnp.sum(lp[0:1] * lp[1:2], axis=1, keepdims=True)
    s2 = jnp.sum(lp[2:3] * lp[3:4], axis=1, keepdims=True)
    return jnp.exp(s1) - jnp.exp(s2) + LAMBDA_INIT


def _stack_sub_queries(q):
    lane = lax.broadcasted_iota(jnp.int32, q.shape, 1)
    zero = jnp.zeros_like(q)
    return jnp.concatenate([jnp.where(lane < ATTN_DK, q, zero), jnp.where(lane >= ATTN_DK, q, zero)], axis=0)


def _lane_tile(x, n):
    return x if n == 1 else jnp.concatenate([x] * n, axis=1)


def _flash_update(s, v, m_sc, acc_sc):
    tk = s.shape[1]
    m_prev = m_sc[...]
    m_new = jnp.maximum(m_prev, jnp.max(s, axis=1, keepdims=True))
    p = jnp.exp2(s - _lane_tile(m_new, tk // LANES))
    alpha = jnp.exp2(m_prev - m_new)
    acc_sc[...] = _lane_tile(alpha, 2) * acc_sc[...] + jnp.dot(p.astype(BF16), v, preferred_element_type=F32)
    m_sc[...] = m_new


def _attn_prompt_kernel(t, lamp_ref, q_ref, k_ref, v_ref, o_ref, q2_sc, m_sc, acc_sc):
    iq = pl.program_id(2)
    q2_sc[...] = _stack_sub_queries(q_ref[0])
    m_sc[...] = jnp.full_like(m_sc, NEG_BIG)
    acc_sc[...] = jnp.zeros_like(acc_sc)

    def scores(ik):
        k0 = pl.multiple_of(ik * t, t)
        s = lax.dot_general(q2_sc[...], k_ref[0, pl.ds(k0, t), :], (((1,), (1,)), ((), ())),
                            preferred_element_type=F32)
        return s, v_ref[0, pl.ds(k0, t), :]

    def causal(s):
        row = lax.broadcasted_iota(jnp.int32, s.shape, 0)
        col = lax.broadcasted_iota(jnp.int32, s.shape, 1)
        return jnp.where(col <= jnp.where(row >= t, row - t, row), s, NEG_BIG)

    def two_tiles(ia, ib, diag_b):
        sa, va = scores(ia)
        sb, vb = scores(ib)
        _flash_update(sa, va, m_sc, acc_sc)
        _flash_update(causal(sb) if diag_b else sb, vb, m_sc, acc_sc)

    def pair_body(i, c):
        two_tiles(2 * i, 2 * i + 1, False)
        return c

    lax.fori_loop(0, iq // 2, pair_body, 0)

    @pl.when(iq % 2 == 1)
    def _():
        two_tiles(iq - 1, iq, True)

    @pl.when(iq % 2 == 0)
    def _():
        s, v = scores(iq)
        _flash_update(causal(s), v, m_sc, acc_sc)

    lam = _lambda_value(lamp_ref)
    acc = acc_sc[...]
    o = acc[:, :ATTN_DV] / acc[:, ATTN_DV:]
    o_ref[0] = o[:t] - lam * o[t:]


def _attn_prompt(lamp, qb, kb, vb, t):
    b, lp, _ = qb.shape
    return pl.pallas_call(
        functools.partial(_attn_prompt_kernel, t),
        grid=(b, N_HEADS, lp // t),
        in_specs=[pl.BlockSpec(lamp.shape, lambda i, h, j: (0, 0)),
                  pl.BlockSpec((1, t, ATTN_DV), lambda i, h, j: (i, j, h)),
                  pl.BlockSpec((1, lp, ATTN_DV), lambda i, h, j: (i, 0, h)),
                  pl.BlockSpec((1, lp, 2 * ATTN_DV), lambda i, h, j: (i, 0, h))],
        out_specs=pl.BlockSpec((1, t, ATTN_DV), lambda i, h, j: (i, j, h)),
        out_shape=jax.ShapeDtypeStruct((b, lp, ATTN_WIDTH), F32),
        scratch_shapes=[pltpu.VMEM((2 * t, ATTN_DV), BF16),
                        pltpu.VMEM((2 * t, LANES), F32),
                        pltpu.VMEM((2 * t, 2 * ATTN_DV), F32)],
        compiler_params=_cparams(("parallel", "parallel", "parallel")),
        name="attn_prompt",
    )(lamp, qb, kb, vb)


def _attn_sample_kernel(n_pages, s_len, pt_ref, lamp_ref, q_ref, kn_ref, vn_ref, *refs):
    kt_refs = refs[:n_pages]
    v_refs = refs[n_pages:2 * n_pages]
    o_ref, q2_sc, m_sc, l_sc, acc_sc = refs[2 * n_pages:]
    j = pl.program_id(1)
    hr = 2 * s_len

    @pl.when(j == 0)
    def _():
        q = q_ref[0]
        lane = lax.broadcasted_iota(jnp.int32, q.shape, 1)
        zero = jnp.zeros_like(q)
        q2_sc[...] = jnp.concatenate(
            [jnp.where((lane >= lo) & (lane < lo + ATTN_DK), q, zero) for lo in range(0, ATTN_WIDTH, ATTN_DK)], axis=0)
        m_sc[...] = jnp.full_like(m_sc, NEG_BIG)
        l_sc[...] = jnp.zeros_like(l_sc)
        acc_sc[...] = jnp.zeros_like(acc_sc)

    def update(s, values):
        tk = s.shape[1]
        m_prev = m_sc[...]
        m_new = jnp.maximum(m_prev, jnp.max(s, axis=1, keepdims=True))
        p = jnp.exp2(s - _lane_tile(m_new, tk // LANES))
        alpha = jnp.exp2(m_prev - m_new)
        l_sc[...] = alpha * l_sc[...] + jnp.sum(p, axis=1, keepdims=True)
        pb = p.astype(BF16)
        for h in range(N_HEADS):
            rs = slice(h * hr, (h + 1) * hr)
            acc_sc[rs, :] = alpha[rs] * acc_sc[rs, :] + jnp.dot(pb[rs], values(h), preferred_element_type=F32)
        m_sc[...] = m_new

    kt = jnp.concatenate([r[0] for r in kt_refs], axis=1).astype(BF16)
    s = jnp.dot(q2_sc[...], kt, preferred_element_type=F32)
    update(s, lambda h: jnp.concatenate([r[0, pl.ds(h, LANES, stride=N_HEADS), :] for r in v_refs],
                                        axis=0).astype(BF16))

    @pl.when(j == pl.num_programs(1) - 1)
    def _():
        lam = _lambda_value(lamp_ref)
        pad_k = jnp.zeros((LANES - s_len, ATTN_WIDTH), BF16)
        pad_v = jnp.zeros((LANES - s_len, ATTN_DV), BF16)
        s_own = lax.dot_general(q2_sc[...], jnp.concatenate([kn_ref[0], pad_k], axis=0),
                                (((1,), (1,)), ((), ())), preferred_element_type=F32)
        row = lax.broadcasted_iota(jnp.int32, s_own.shape, 0)
        col = lax.broadcasted_iota(jnp.int32, s_own.shape, 1)
        s_own = jnp.where(col <= (row & (s_len - 1)), s_own, NEG_BIG)
        update(s_own, lambda h: jnp.concatenate([vn_ref[0, :, 2 * h * ATTN_DV:(2 * h + 1) * ATTN_DV], pad_v], axis=0))
        o = acc_sc[...] / l_sc[...]
        for h in range(N_HEADS):
            o_ref[0, :, h * ATTN_DV:(h + 1) * ATTN_DV] = (o[h * hr:h * hr + s_len]
                                                          - lam * o[h * hr + s_len:(h + 1) * hr])


def _attn_sample(lamp, qb, kb, vb, cache_kt, cache_v, page_table, n_pages):
    db, s_len, _ = qb.shape
    n_past_pages = page_table.shape[1]
    assert cache_kt.shape[2] == LANES and s_len & (s_len - 1) == 0 and n_past_pages % n_pages == 0
    steps = n_past_pages // n_pages
    tok = lambda w: pl.BlockSpec((1, s_len, w), lambda i, j, pt: (i, 0, 0))

    def page_spec(p):
        return pl.BlockSpec((1, ATTN_WIDTH, LANES), lambda i, j, pt: (pt[i, j * n_pages + p], 0, 0))

    rows = 2 * s_len * N_HEADS
    return pl.pallas_call(
        functools.partial(_attn_sample_kernel, n_pages, s_len),
        grid_spec=pltpu.PrefetchScalarGridSpec(
            num_scalar_prefetch=1,
            grid=(db, steps),
            in_specs=[pl.BlockSpec(lamp.shape, lambda i, j, pt: (0, 0)),
                      tok(ATTN_WIDTH), tok(ATTN_WIDTH), tok(2 * ATTN_WIDTH)]
                     + [page_spec(p) for p in range(n_pages)] * 2,
            out_specs=tok(ATTN_WIDTH),
            scratch_shapes=[pltpu.VMEM((rows, ATTN_WIDTH), BF16),
                            pltpu.VMEM((rows, LANES), F32),
                            pltpu.VMEM((rows, LANES), F32),
                            pltpu.VMEM((rows, ATTN_DV), F32)]),
        out_shape=jax.ShapeDtypeStruct((db, s_len, ATTN_WIDTH), F32),
        compiler_params=_cparams(("parallel", "arbitrary")),
        name="attn_sample",
    )(page_table, lamp, qb, kb, vb, *([cache_kt] * n_pages), *([cache_v] * n_pages))


def _finish_kernel(h_ref, ys_ref, o_ref, sg_ref, wout_ref, g2_ref, wq_ref, keys_ref,
                   h1_ref, n2_ref, st_ref):
    o = o_ref[0]
    parts = []
    for h in range(N_HEADS):
        oh = o[:, h * ATTN_DV:(h + 1) * ATTN_DV]
        parts.append(oh * lax.rsqrt(jnp.mean(oh * oh, axis=-1, keepdims=True) + EPS))
    on = jnp.concatenate(parts, axis=1) * sg_ref[...] * (1.0 - LAMBDA_INIT)
    mix = jnp.concatenate([ys_ref[0], on.astype(BF16)], axis=1)
    h1 = h_ref[0] + jnp.dot(mix, wout_ref[...], preferred_element_type=F32)
    h1_ref[0] = h1
    n2 = (h1 * lax.rsqrt(jnp.mean(h1 * h1, axis=-1, keepdims=True) + EPS) * g2_ref[...]).astype(BF16)
    n2_ref[0] = n2
    qt = lax.dot_general(wq_ref[...], n2, (((1,), (1,)), ((), ())), preferred_element_type=F32)
    for hp in range(2 * PEER_HEADS):
        st_ref[0, hp] = jnp.dot(keys_ref[hp], qt[hp * N_KEYS:(hp + 1) * N_KEYS].astype(BF16),
                                preferred_element_type=F32)


def _finish(h, ys, o, tm, sg, wout, g2, wq_t, keys):
    b, lp, _ = h.shape
    nblk = lp // tm
    tok = lambda w: pl.BlockSpec((1, tm, w), lambda i, j: (i, j, 0))
    full = lambda a: pl.BlockSpec(a.shape, lambda i, j: (0,) * a.ndim)
    return pl.pallas_call(
        _finish_kernel,
        grid=(b, nblk),
        in_specs=[tok(D_MODEL), tok(SSM_WIDTH), tok(ATTN_WIDTH), full(sg), full(wout), full(g2),
                  full(wq_t), full(keys)],
        out_specs=[tok(D_MODEL), tok(D_MODEL),
                   pl.BlockSpec((1, 2 * PEER_HEADS, N_KEYS, tm), lambda i, j: (i, 0, 0, j))],
        out_shape=[jax.ShapeDtypeStruct((b, lp, D_MODEL), F32),
                   jax.ShapeDtypeStruct((b, lp, D_MODEL), BF16),
                   jax.ShapeDtypeStruct((b, 2 * PEER_HEADS, N_KEYS, lp), F32)],
        compiler_params=_cparams(("parallel", "parallel")),
        name="finish",
    )(h, ys, o, sg, wout, g2, wq_t, keys)


def _cmpx(v, i, j):
    hi = jnp.maximum(v[i], v[j])
    lo = jnp.minimum(v[i], v[j])
    v[i], v[j] = hi, lo


def _bitonic_merge_desc(v):
    n = len(v)
    v = list(v)
    d = n // 2
    while d >= 1:
        for i in range(n):
            if (i % (2 * d)) < d:
                _cmpx(v, i, i + d)
        d //= 2
    return v


def _sort_desc(v):
    n = len(v)
    if n == 1:
        return list(v)
    a = _sort_desc(v[:n // 2])
    b = _sort_desc(v[n // 2:])
    return _bitonic_merge_desc(a + b[::-1])


def _merge_top(t, s):
    n = len(t)
    m = list(t)
    for r in range(n):
        q = n - 1 - r
        if q < len(s):
            m[r] = jnp.maximum(t[r], s[q])
    return _bitonic_merge_desc(m)


def _pack_pair(lo, hi):
    lo_bits = lax.bitcast_convert_type(lo.astype(BF16).astype(F32), jnp.uint32)
    hi_bits = lax.bitcast_convert_type(hi.astype(BF16).astype(F32), jnp.uint32)
    return lax.bitcast_convert_type(hi_bits | (lo_bits >> 16), F32)


def _twice_bf16(x):
    return _pack_pair(x, x)


def _peer_select_kernel(s_ref, cnt_ref, w1_ref, rank_ref, e2_ref):
    tb = s_ref.shape[-1]
    sub = lax.broadcasted_iota(jnp.int32, (SUBLANES, LANES), 0)
    k = PEER_TOPK

    def lane_group(ln):
        tops = []
        for hp in range(2 * PEER_HEADS):
            col = _sort_desc([s_ref[0, hp, r * SUBLANES:(r + 1) * SUBLANES, ln] for r in range(N_KEYS // SUBLANES)])
            for shift in (4, 2, 1):
                other = [pltpu.roll(c, shift, 0) for c in col]
                col = _merge_top(col, other)
            tops.append(col)

        def pack(lists):
            out = []
            for r in range(k):
                x = lists[0][r]
                for h in range(1, PEER_HEADS):
                    x = jnp.where(sub == h, lists[h][r], x)
                out.append(x)
            return out

        a = pack([tops[2 * h] for h in range(PEER_HEADS)])
        b = pack([tops[2 * h + 1] for h in range(PEER_HEADS)])
        t = [a[0] + b[q] for q in range(k)]
        for i in range(1, k // 2):
            t = _merge_top(t, [a[i] + b[q] for q in range(k // (i + 1))])
        t = _merge_top(t, [a[i] + b[0] for i in range(k // 2, k)])
        theta = t[k - 1]
        z = jnp.ones_like(theta)
        for r in range(1, k):
            z = z + jnp.exp(t[r] - t[0])
        inv_z = 1.0 / z
        counts = []
        for r in range(k):
            cnt = jnp.zeros_like(theta)
            for q in range(k // (r + 1)):
                cnt = cnt + jnp.where(a[r] + b[q] >= theta, 1.0, 0.0)
            counts.append(cnt)

        for h in range(PEER_HEADS):
            bc = lambda x: jnp.broadcast_to(x[h:h + 1, :], (SUBLANES, LANES))
            a_h = tops[2 * h]
            b_h = tops[2 * h + 1]
            cnt_h = [bc(x) for x in counts]
            inv_z_h = bc(inv_z)
            for r in range(N_KEYS // SUBLANES):
                rs = slice(r * SUBLANES, (r + 1) * SUBLANES)
                s1 = s_ref[0, 2 * h, rs, ln]
                cnt = jnp.zeros_like(s1)
                for q in range(k):
                    cnt = jnp.where(s1 == a_h[q], cnt_h[q], cnt)
                cnt_ref[0, h, rs, ln] = _twice_bf16(cnt)
                w1_ref[0, h, rs, ln] = _twice_bf16(jnp.exp(s1 - a_h[0]) * inv_z_h)

            def second_key(start):
                s2 = s_ref[0, 2 * h + 1, pl.ds(start, SUBLANES, stride=2), ln]
                rank = jnp.full_like(s2, float(N_KEYS - 1))
                for q in range(k):
                    rank = jnp.where(s2 == b_h[q], float(q), rank)
                return rank, jnp.exp(s2 - b_h[0])

            for g in range(N_KEYS // BF16_ROWS):
                rank_even, e2_even = second_key(g * BF16_ROWS)
                rank_odd, e2_odd = second_key(g * BF16_ROWS + 1)
                ws = slice(g * SUBLANES, (g + 1) * SUBLANES)
                rank_ref[0, h, ws, ln] = _pack_pair(rank_even, rank_odd)
                e2_ref[0, h, ws, ln] = _pack_pair(e2_even, e2_odd)

    for g in range(tb // LANES):
        lane_group(slice(g * LANES, (g + 1) * LANES))


def _peer_select(st):
    tb = LANES
    b, _, _, lp = st.shape
    out = lambda rows: jax.ShapeDtypeStruct((b, PEER_HEADS, rows, lp), F32)
    spec = lambda rows: pl.BlockSpec((1, PEER_HEADS, rows, tb), lambda i, j: (i, 0, 0, j))
    return pl.pallas_call(
        _peer_select_kernel,
        grid=(b, lp // tb),
        in_specs=[pl.BlockSpec((1, 2 * PEER_HEADS, N_KEYS, tb), lambda i, j: (i, 0, 0, j))],
        out_specs=[spec(N_KEYS), spec(N_KEYS), spec(N_KEYS // 2), spec(N_KEYS // 2)],
        out_shape=[out(N_KEYS), out(N_KEYS), out(N_KEYS // 2), out(N_KEYS // 2)],
        compiler_params=_cparams(("parallel", "parallel")),
        name="peer_select",
    )(st)


def _peer_dense_kernel(ec, n2_ref, u_ref, vt_ref, rank_ref, e2_ref, cnt_ref, w1_ref, h1_ref,
                       out_ref, a_sc, w_sc, acc_sc):
    c = pl.program_id(2)
    tb = n2_ref.shape[1]

    @pl.when(c == 0)
    def _():
        acc_sc[...] = jnp.zeros_like(acc_sc)

    n2 = n2_ref[0]
    n_sub = ec // PEER_SUB

    def activations(sb):
        rows = slice(sb * PEER_SUB, (sb + 1) * PEER_SUB)
        a_sc[rows, :] = lax.dot_general(u_ref[rows, :], n2, (((1,), (1,)), ((), ())), preferred_element_type=F32)

    def gate(sb):
        il0 = sb * (PEER_SUB // N_KEYS)
        for cc in range(tb // LANES):
            ln = slice(cc * LANES, (cc + 1) * LANES)
            bc = lambda ref, h, il: pltpu.bitcast(jnp.broadcast_to(ref[0, h, il:il + 1, ln], (SUBLANES, LANES)), BF16)
            cnt = [[bc(cnt_ref, h, il0 + d) for h in range(PEER_HEADS)] for d in range(2)]
            w1 = [[bc(w1_ref, h, il0 + d) for h in range(PEER_HEADS)] for d in range(2)]
            for jp in range(N_KEYS // BF16_ROWS):
                ws = slice(jp * SUBLANES, (jp + 1) * SUBLANES)
                zero = jnp.zeros((BF16_ROWS, LANES), BF16)
                g = [zero, zero]
                for h in range(PEER_HEADS):
                    rank = pltpu.bitcast(rank_ref[0, h, ws, ln], BF16)
                    e2 = pltpu.bitcast(e2_ref[0, h, ws, ln], BF16)
                    for d in range(2):
                        g[d] = g[d] + jnp.where(rank < cnt[d][h], e2 * w1[d][h], zero)
                for d in range(2):
                    r0 = (il0 + d) * N_KEYS + jp * BF16_ROWS
                    w = g[d] * _gelu(a_sc[r0:r0 + BF16_ROWS, ln]).astype(BF16)
                    w_sc[r0 // 2:r0 // 2 + SUBLANES, ln] = pltpu.bitcast(w, F32)

    part = None
    activations(0)
    for sb in range(n_sub):
        gate(sb)
        if sb + 1 < n_sub:
            activations(sb + 1)
        rows = slice(sb * PEER_SUB, (sb + 1) * PEER_SUB)
        words = w_sc[sb * PEER_SUB // 2:(sb + 1) * PEER_SUB // 2, :]
        d = jnp.dot(vt_ref[:, rows], pltpu.bitcast(words, BF16), preferred_element_type=F32)
        part = d if part is None else part + d
    acc_sc[...] += part

    @pl.when(c == pl.num_programs(2) - 1)
    def _():
        out_ref[0] = h1_ref[0] + acc_sc[...].T


def _peer_dense(n2, u_bf, vt_bf, rank, e2, cnt, w1, h1, tb, ec):
    b, lp, _ = n2.shape
    rows = ec // N_KEYS
    keyed = pl.BlockSpec((1, PEER_HEADS, N_KEYS // 2, tb), lambda i, j, c: (i, 0, 0, j))
    chunk_rows = pl.BlockSpec((1, PEER_HEADS, rows, tb), lambda i, j, c: (i, 0, c, j))
    return pl.pallas_call(
        functools.partial(_peer_dense_kernel, ec),
        grid=(b, lp // tb, N_EXPERTS // ec),
        in_specs=[pl.BlockSpec((1, tb, D_MODEL), lambda i, j, c: (i, j, 0)),
                  pl.BlockSpec((ec, D_MODEL), lambda i, j, c: (c, 0)),
                  pl.BlockSpec((D_MODEL, ec), lambda i, j, c: (0, c)),
                  keyed, keyed, chunk_rows, chunk_rows,
                  pl.BlockSpec((1, tb, D_MODEL), lambda i, j, c: (i, j, 0))],
        out_specs=pl.BlockSpec((1, tb, D_MODEL), lambda i, j, c: (i, j, 0)),
        out_shape=jax.ShapeDtypeStruct((b, lp, D_MODEL), F32),
        scratch_shapes=[pltpu.VMEM((ec, tb), F32), pltpu.VMEM((ec // 2, tb), F32), pltpu.VMEM((D_MODEL, tb), F32)],
        compiler_params=_cparams(("parallel", "parallel", "arbitrary")),
        name="peer_dense",
    )(n2, u_bf, vt_bf, rank, e2, cnt, w1, h1)


def _rope_tables(pos):
    half = ATTN_DK // 2
    inv_freq = ROPE_THETA ** (-jnp.arange(half, dtype=F32) * 2.0 / ATTN_DK)
    ang = pos[:, None] * inv_freq[None, :]
    cos = jnp.cos(ang)
    sin = jnp.sin(ang)
    return (jnp.concatenate([cos, cos, cos, cos], axis=1),
            jnp.concatenate([-sin, sin, -sin, sin], axis=1))


def _round_up(x, m):
    return (x + m - 1) // m * m


def _layer(h, l_real, pos, weights, tm, tb, ec, attend):
    (g1, w_in, qg, kg, gm, ssm_c, sg, wout, g2, wq_t, keys, u_bf, vt_bf) = weights
    cos_t, sin_t = _rope_tables(pos)
    u, k, v, qb, kb, vb = _project(h, l_real, tm, g1, w_in, qg, kg, cos_t, sin_t, gm)
    o, ys, st = attend(u, qb, kb, vb, ssm_c)
    h1, n2, sc = _finish(h, ys, o, tb, sg, wout, g2, wq_t, keys)
    cnt, w1, rank, e2 = _peer_select(sc)
    out = _peer_dense(n2, u_bf, vt_bf, rank, e2, cnt, w1, h1, tb, ec)
    return out, k, v, st


def kernel(x_prompt, x_sample, cache_k, cache_v, state_ssm_re, state_ssm_im, page_table, meta_tokens, norm1_g, w_in, q_norm_g, k_norm_g, lambda_q1, lambda_k1, lambda_q2, lambda_k2, subln_g, ssm_A_re, ssm_A_im, ssm_log_dt, ssm_B_re, ssm_B_im, ssm_C_re, ssm_C_im, ssm_D, w_glu, b_glu, w_out, norm2_g, peer_w_q, peer_keys, peer_u, peer_v):
    assert w_in.shape[0] == 1, "single-layer trunk"
    bp, seq, _ = x_prompt.shape
    db, ds, _ = x_sample.shape
    lp_real = seq + N_META
    assert lp_real % SUBLANES == 0 and ds == SUBLANES
    tm_p, t_attn, tb, ec = 768, 512, 512, 1024
    lp = _round_up(lp_real, math.lcm(tm_p, t_attn, tb))
    past = page_table.shape[1] * cache_k.shape[2]

    gm = jnp.kron(jnp.eye(ATTN_WIDTH // ATTN_DK, dtype=F32),
                  jnp.full((ATTN_DK, ATTN_DK), 1.0 / ATTN_DK, F32)).astype(BF16)
    lamp = jnp.stack([lambda_q1[0], lambda_k1[0], lambda_q2[0], lambda_k2[0]]).astype(F32)
    ssm_c = _ssm_consts(ssm_A_re[0], ssm_A_im[0], ssm_log_dt[0], ssm_B_re[0], ssm_B_im[0],
                        ssm_C_re[0], ssm_C_im[0], ssm_D[0], w_glu[0], b_glu[0])
    weights = (norm1_g[0].astype(F32)[None, :], w_in[0].astype(BF16),
               jnp.tile(q_norm_g[0].astype(F32), ATTN_WIDTH // ATTN_DK)[None, :],
               jnp.tile(k_norm_g[0].astype(F32), ATTN_WIDTH // ATTN_DK)[None, :],
               gm, ssm_c,
               jnp.tile(subln_g[0].astype(F32), N_HEADS)[None, :], w_out[0].astype(BF16),
               norm2_g[0].astype(F32)[None, :], peer_w_q[0].T.astype(BF16),
               peer_keys[0].reshape(2 * PEER_HEADS, N_KEYS, N_KEYS).astype(BF16),
               peer_u[0].astype(BF16), peer_v[0].T.astype(BF16))

    hp = jnp.concatenate([jnp.broadcast_to(meta_tokens.astype(F32)[None], (bp, N_META, D_MODEL)),
                          x_prompt.astype(F32),
                          jnp.zeros((bp, lp - lp_real, D_MODEL), F32)], axis=1)

    def attend_prompt(u, qb, kb, vb, consts):
        ys, st = _ssm(u, tm_p, lp_real - 1, consts)
        return _attn_prompt(lamp, qb, kb, vb, t_attn), ys, st

    hp_out, k_p, v_p, st_p = _layer(hp, lp_real, jnp.arange(lp, dtype=F32), weights, tm_p, tb, ec, attend_prompt)

    n_tok = db * ds
    hs = x_sample.astype(F32).reshape(1, n_tok, D_MODEL)
    h0 = jnp.concatenate([state_ssm_re[0].reshape(db, SSM_FLAT), state_ssm_im[0].reshape(db, SSM_FLAT)],
                         axis=1).astype(F32)[None]
    n_pool, page = cache_k.shape[1], cache_k.shape[2]
    ckt = jnp.transpose(cache_k[0], (0, 2, 3, 4, 1)).reshape(n_pool, ATTN_WIDTH, page)
    cv = cache_v[0].reshape(n_pool, page * N_HEADS, ATTN_DV)

    def attend_sample(u, qb, kb, vb, consts):
        ys, st = _ssm(u, 256, None, consts, h0=h0)
        seqs = lambda a: a.reshape(db, ds, a.shape[-1])
        o = _attn_sample(lamp, seqs(qb), seqs(kb), seqs(vb), ckt, cv, page_table, 16)
        return o.reshape(1, n_tok, ATTN_WIDTH), ys, st

    pos_s = jnp.tile(past + jnp.arange(ds, dtype=F32), db)
    hs_out, k_s, v_s, st_s = _layer(hs, n_tok, pos_s, weights, 512, tb, ec, attend_sample)

    y_prompt = hp_out[:, N_META:lp_real].astype(x_prompt.dtype)
    y_sample = hs_out.reshape(db, ds, D_MODEL).astype(x_sample.dtype)
    row = (lp_real - 1) % SUBLANES
    kd, vd, sd = cache_k.dtype, cache_v.dtype, state_ssm_re.dtype
    return (y_prompt, y_sample,
            k_p.reshape(1, bp, lp_real, N_HEADS, 2, ATTN_DK).astype(kd),
            v_p.reshape(1, bp, lp_real, N_HEADS, ATTN_DV).astype(vd),
            st_p[:, row, :SSM_FLAT].reshape(1, bp, SSM_GROUPS, SSM_STATE).astype(sd),
            st_p[:, row, SSM_FLAT:].reshape(1, bp, SSM_GROUPS, SSM_STATE).astype(state_ssm_im.dtype),
            k_s.reshape(1, db, ds, N_HEADS, 2, ATTN_DK).astype(kd),
            v_s.reshape(1, db, ds, N_HEADS, ATTN_DV).astype(vd),
            st_s[0, ds - 1::ds, :SSM_FLAT].reshape(1, db, SSM_GROUPS, SSM_STATE).astype(sd),
            st_s[0, ds - 1::ds, SSM_FLAT:].reshape(1, db, SSM_GROUPS, SSM_STATE).astype(state_ssm_im.dtype))
```

```python
import functools
import math

import jax
import jax.numpy as jnp
from jax import lax
from jax.experimental import pallas as pl
from jax.experimental.pallas import tpu as pltpu

F32 = jnp.float32
BF16 = jnp.bfloat16

D_MODEL = 1024
N_META = 16
SSM_WIDTH = 512
SSM_GROUP = 16
SSM_GROUPS = 32
SSM_STATE = 64
SSM_FLAT = SSM_GROUPS * SSM_STATE
ATTN_WIDTH = 512
ATTN_DV = 128
N_HEADS = 4
ATTN_DK = 64
IN_WIDTH = 2048
ROPE_THETA = 10000.0
N_KEYS = 128
N_EXPERTS = N_KEYS * N_KEYS
PEER_HEADS = 8
PEER_TOPK = 16
PEER_SUB = 2 * N_KEYS
PEER_CHUNK = 8 * N_KEYS
EPS = 1e-6
NEG_BIG = -1e30
POS_BIG = 3e38
LAMBDA_INIT = 0.8 - 0.6 * math.exp(-0.3 * 0)
Q_SCALE = ATTN_DK ** -0.5 * math.log2(math.e)
ATTN_GROUP = 4

SUBLANES = 8
LANES = 128
BF16_ROWS = 2 * SUBLANES
VMEM_LIMIT = 56 * 1024 * 1024


def _cparams(sem):
    return pltpu.CompilerParams(dimension_semantics=sem, vmem_limit_bytes=VMEM_LIMIT)


GELU_C1 = 0.7978845608028654
GELU_C2 = GELU_C1 * 0.044715


def _gelu(x):
    return x * (0.5 + 0.5 * jnp.tanh(x * (GELU_C1 + GELU_C2 * (x * x))))


def _proj_kernel(x_ref, g1_ref, w_ref, qg_ref, kg_ref, cos_ref, sin_ref, gm_ref,
                 u_ref, k_ref, v_ref, qb_ref, kb_ref, vb_ref):
    x = x_ref[0]
    tm = x.shape[0]
    ms = jnp.mean(x * x, axis=-1, keepdims=True)
    n = (x * lax.rsqrt(ms + EPS) * g1_ref[...]).astype(BF16)
    z = jnp.dot(n, w_ref[...], preferred_element_type=F32)
    u_ref[0] = z[:, :SSM_WIDTH]
    v = z[:, SSM_WIDTH + 2 * ATTN_WIDTH:]
    v_ref[0] = v
    vb = v.astype(BF16)
    ones = jnp.ones((tm, ATTN_DV), BF16)
    vb_ref[0] = jnp.concatenate([t for h in range(N_HEADS) for t in (vb[:, h * ATTN_DV:(h + 1) * ATTN_DV], ones)], axis=1)

    cos = jnp.concatenate([cos_ref[...]] * 4, axis=1)
    sin = jnp.concatenate([sin_ref[...]] * 4, axis=1)
    lane = lax.broadcasted_iota(jnp.int32, (tm, ATTN_WIDTH), 1)
    first_half = (lane & (ATTN_DK - 1)) < (ATTN_DK // 2)
    gm = gm_ref[...]

    def norm_rope(t, g):
        sq = t * t
        hi = sq.astype(BF16)
        lo = (sq - hi.astype(F32)).astype(BF16)
        ms64 = (jnp.dot(hi, gm, preferred_element_type=F32)
                + jnp.dot(lo, gm, preferred_element_type=F32))
        tn = t * lax.rsqrt(ms64 + EPS) * g
        swapped = jnp.where(first_half,
                            pltpu.roll(tn, ATTN_WIDTH - ATTN_DK // 2, 1),
                            pltpu.roll(tn, ATTN_DK // 2, 1))
        return tn * cos + swapped * sin

    q = norm_rope(z[:, SSM_WIDTH:SSM_WIDTH + ATTN_WIDTH], qg_ref[...])
    k = norm_rope(z[:, SSM_WIDTH + ATTN_WIDTH:SSM_WIDTH + 2 * ATTN_WIDTH], kg_ref[...])
    k_ref[0] = k
    kb_ref[0] = k.astype(BF16)
    qb_ref[0] = (q * Q_SCALE).astype(BF16)


def _project(h, l_out, tm, g1, w_bf, qg, kg, cos_t, sin_t, gm):
    b, lp, _ = h.shape
    nblk = lp // tm
    tok = lambda w: pl.BlockSpec((1, tm, w), lambda i, j: (i, j, 0))
    full = lambda a: pl.BlockSpec(a.shape, lambda i, j: (0,) * a.ndim)
    return pl.pallas_call(
        _proj_kernel,
        grid=(b, nblk),
        in_specs=[tok(D_MODEL), full(g1), full(w_bf), full(qg), full(kg),
                  pl.BlockSpec((tm, LANES), lambda i, j: (j, 0)),
                  pl.BlockSpec((tm, LANES), lambda i, j: (j, 0)), full(gm)],
        out_specs=[tok(SSM_WIDTH), tok(ATTN_WIDTH), tok(ATTN_WIDTH),
                   tok(ATTN_WIDTH), tok(ATTN_WIDTH), tok(2 * ATTN_WIDTH)],
        out_shape=[jax.ShapeDtypeStruct((b, lp, SSM_WIDTH), F32),
                   jax.ShapeDtypeStruct((b, l_out, ATTN_WIDTH), F32),
                   jax.ShapeDtypeStruct((b, l_out, ATTN_WIDTH), F32),
                   jax.ShapeDtypeStruct((b, lp, ATTN_WIDTH), BF16),
                   jax.ShapeDtypeStruct((b, lp, ATTN_WIDTH), BF16),
                   jax.ShapeDtypeStruct((b, lp, 2 * ATTN_WIDTH), BF16)],
        compiler_params=_cparams(("parallel", "parallel")),
        name="project",
    )(h, g1, w_bf, qg, kg, cos_t, sin_t, gm)


SSM_CHUNK = 512


def _ssm_kernel(sequential, state_row, state_blk, *refs):
    if sequential:
        (u_ref, bbd_ref, cbd_ref, kc_ref, pc_ref, d_ref, wg_ref, bg_ref,
         y_ref, st_ref, x_sc, carry_sc) = refs
        h0_ref = None
    else:
        (u_ref, bbd_ref, cbd_ref, kc_ref, pc_ref, d_ref, wg_ref, bg_ref, h0_ref,
         y_ref, st_ref, x_sc) = refs
        carry_sc = None
    j = pl.program_id(1)
    u = u_ref[0]
    tm = u.shape[0]
    ub = u.astype(BF16)
    hw, hf = SSM_WIDTH // 2, SSM_FLAT // 2
    for half in range(2):
        xh = jnp.dot(ub[:, half * hw:(half + 1) * hw], bbd_ref[half], preferred_element_type=F32)
        x_sc[:, half * hf:(half + 1) * hf] = xh[:, :hf]
        x_sc[:, SSM_FLAT + half * hf:SSM_FLAT + (half + 1) * hf] = xh[:, hf:]

    if sequential:
        @pl.when(j == 0)
        def _():
            carry_sc[...] = jnp.zeros_like(carry_sc)

    def tile_body(i, carry):
        r0 = pl.multiple_of(i * SUBLANES, SUBLANES)
        for c in range(0, SSM_FLAT, SSM_CHUNK):
            re = slice(c, c + SSM_CHUNK)
            im = slice(SSM_FLAT + c, SSM_FLAT + c + SSM_CHUNK)
            xr = x_sc[pl.ds(r0, SUBLANES), re]
            xi = x_sc[pl.ds(r0, SUBLANES), im]
            for di, d in enumerate((1, 2, 4)):
                cr = kc_ref[di, 0, :, re]
                ci = kc_ref[di, 1, :, re]
                rr = pltpu.roll(xr, d, 0)
                ri = pltpu.roll(xi, d, 0)
                xr, xi = xr + cr * rr - ci * ri, xi + cr * ri + ci * rr
            if sequential:
                car_r = carry_sc[:, re]
                car_i = carry_sc[:, im]
            else:
                car_r = h0_ref[0, pl.ds(r0, SUBLANES), re]
                car_i = h0_ref[0, pl.ds(r0, SUBLANES), im]
            pr = pc_ref[0, :, re]
            pi = pc_ref[1, :, re]
            xr, xi = xr + pr * car_r - pi * car_i, xi + pr * car_i + pi * car_r
            x_sc[pl.ds(r0, SUBLANES), re] = xr
            x_sc[pl.ds(r0, SUBLANES), im] = xi
            if sequential:
                carry_sc[:, re] = jnp.broadcast_to(xr[SUBLANES - 1:, :], (SUBLANES, SSM_CHUNK))
                carry_sc[:, im] = jnp.broadcast_to(xi[SUBLANES - 1:, :], (SUBLANES, SSM_CHUNK))
        return carry

    lax.fori_loop(0, tm // SUBLANES, tile_body, 0)

    if sequential:
        @pl.when(j == state_blk)
        def _():
            st_ref[0] = x_sc[state_row:state_row + SUBLANES, :]
    else:
        st_ref[0] = x_sc[...]

    ys = []
    for half in range(2):
        x_re = x_sc[:, half * hf:(half + 1) * hf].astype(BF16)
        x_im = x_sc[:, SSM_FLAT + half * hf:SSM_FLAT + (half + 1) * hf].astype(BF16)
        ys.append(jnp.dot(x_re, cbd_ref[half, :hf], preferred_element_type=F32)
                  + jnp.dot(x_im, cbd_ref[half, hf:], preferred_element_type=F32))
    y = jnp.concatenate(ys, axis=1) + d_ref[...] * u
    y = _gelu(y)
    gate = jnp.dot(y.astype(BF16), wg_ref[...], preferred_element_type=F32) + bg_ref[...]
    y_ref[0] = (y * (1.0 / (1.0 + jnp.exp(-gate)))).astype(BF16)


def _ssm(u, tm, last_token, consts, h0=None):
    bbd, cbd, kc, pc, dsk, wg, bg = consts
    b, lp, _ = u.shape
    nblk = lp // tm
    sequential = h0 is None
    full = lambda a: pl.BlockSpec(a.shape, lambda i, j: (0,) * a.ndim)
    in_specs = [pl.BlockSpec((1, tm, SSM_WIDTH), lambda i, j: (i, j, 0)),
                full(bbd), full(cbd), full(kc), full(pc), full(dsk), full(wg), full(bg)]
    args = [u, bbd, cbd, kc, pc, dsk, wg, bg]
    scratch = [pltpu.VMEM((tm, 2 * SSM_FLAT), F32)]
    if sequential:
        state_blk = last_token // tm
        state_row = (last_token % tm) // SUBLANES * SUBLANES
        st_spec = pl.BlockSpec((1, SUBLANES, 2 * SSM_FLAT), lambda i, j: (i, 0, 0))
        st_shape = jax.ShapeDtypeStruct((b, SUBLANES, 2 * SSM_FLAT), F32)
        scratch.append(pltpu.VMEM((SUBLANES, 2 * SSM_FLAT), F32))
        sem = ("parallel", "arbitrary")
    else:
        state_blk = state_row = 0
        in_specs.append(pl.BlockSpec((1, tm, 2 * SSM_FLAT), lambda i, j: (i, j, 0)))
        args.append(jnp.repeat(h0, SUBLANES, axis=1))
        st_spec = pl.BlockSpec((1, tm, 2 * SSM_FLAT), lambda i, j: (i, j, 0))
        st_shape = jax.ShapeDtypeStruct((b, lp, 2 * SSM_FLAT), F32)
        sem = ("parallel", "parallel")
    return pl.pallas_call(
        functools.partial(_ssm_kernel, sequential, state_row, state_blk),
        grid=(b, nblk),
        in_specs=in_specs,
        out_specs=[pl.BlockSpec((1, tm, SSM_WIDTH), lambda i, j: (i, j, 0)), st_spec],
        out_shape=[jax.ShapeDtypeStruct((b, lp, SSM_WIDTH), BF16), st_shape],
        scratch_shapes=scratch,
        compiler_params=_cparams(sem),
        name="s5_prompt" if sequential else "s5_sample",
    )(*args)


def _ssm_consts(a_re, a_im, log_dt, b_re, b_im, c_re, c_im, d_skip, w_glu, b_glu):
    dt = jnp.exp(log_dt.astype(F32))[:, None]
    a_re = a_re.astype(F32)
    a_im = a_im.astype(F32)
    mag = jnp.exp(dt * a_re)
    ab_re = mag * jnp.cos(dt * a_im)
    ab_im = mag * jnp.sin(dt * a_im)
    den = a_re * a_re + a_im * a_im
    z_re = ((ab_re - 1.0) * a_re + ab_im * a_im) / den
    z_im = (ab_im * a_re - (ab_re - 1.0) * a_im) / den
    b_re = b_re.astype(F32)
    b_im = b_im.astype(F32)
    bb_re = z_re[..., None] * b_re - z_im[..., None] * b_im
    bb_im = z_re[..., None] * b_im + z_im[..., None] * b_re
    eye = jnp.eye(SSM_GROUPS, dtype=F32)
    bbd = jnp.concatenate(
        [jnp.einsum('gnc,gh->gchn', bb_re, eye).reshape(SSM_WIDTH, SSM_FLAT),
         jnp.einsum('gnc,gh->gchn', bb_im, eye).reshape(SSM_WIDTH, SSM_FLAT)], axis=1).astype(BF16)
    cbd = jnp.concatenate(
        [jnp.einsum('gcn,gh->gnhc', c_re.astype(F32), eye).reshape(SSM_FLAT, SSM_WIDTH),
         -jnp.einsum('gcn,gh->gnhc', c_im.astype(F32), eye).reshape(SSM_FLAT, SSM_WIDTH)], axis=0).astype(BF16)
    hw, hf = SSM_WIDTH // 2, SSM_FLAT // 2
    half_b = lambda k: jnp.concatenate([bbd[k * hw:(k + 1) * hw, k * hf:(k + 1) * hf],
                                        bbd[k * hw:(k + 1) * hw, SSM_FLAT + k * hf:SSM_FLAT + (k + 1) * hf]], axis=1)
    half_c = lambda k: jnp.concatenate([cbd[k * hf:(k + 1) * hf, k * hw:(k + 1) * hw],
                                        cbd[SSM_FLAT + k * hf:SSM_FLAT + (k + 1) * hf, k * hw:(k + 1) * hw]], axis=0)
    bbd = jnp.stack([half_b(0), half_b(1)])
    cbd = jnp.stack([half_c(0), half_c(1)])

    def power(p):
        m = jnp.exp(p * dt * a_re)
        return (m * jnp.cos(p * dt * a_im)).reshape(-1), (m * jnp.sin(p * dt * a_im)).reshape(-1)

    row = jnp.arange(SUBLANES)[:, None]
    kc = []
    for d in (1, 2, 4):
        pr, pi = power(float(d))
        kc.append(jnp.stack([jnp.where(row >= d, pr[None, :], 0.0), jnp.where(row >= d, pi[None, :], 0.0)]))
    kc = jnp.stack(kc)
    rows = [power(float(s + 1)) for s in range(SUBLANES)]
    pc = jnp.stack([jnp.stack([r[0] for r in rows]), jnp.stack([r[1] for r in rows])])
    return (bbd, cbd, kc, pc, d_skip.astype(F32)[None, :], w_glu.astype(BF16), b_glu.astype(F32)[None, :])


def _lambda_value(lamp_ref):
    lp = lamp_ref[...]
    s1 = jnp.sum(lp[0:1] * lp[1:2], axis=1, keepdims=True)
    s2 = jnp.sum(lp[2:3] * lp[3:4], axis=1, keepdims=True)
    return jnp.exp(s1) - jnp.exp(s2) + LAMBDA_INIT


def _stack_sub_queries(q):
    lane = lax.broadcasted_iota(jnp.int32, q.shape, 1)
    zero = jnp.zeros_like(q)
    return jnp.concatenate([jnp.where(lane < ATTN_DK, q, zero), jnp.where(lane >= ATTN_DK, q, zero)], axis=0)


def _lane_tile(x, n):
    return x if n == 1 else jnp.concatenate([x] * n, axis=1)


def _flash_update(s, v, m_sc, acc_sc):
    tk = s.shape[1]
    m_prev = m_sc[...]
    m_new = jnp.maximum(m_prev, jnp.max(s, axis=1, keepdims=True))
    p = jnp.exp2(s - _lane_tile(m_new, tk // LANES))
    alpha = jnp.exp2(m_prev - m_new)
    acc_sc[...] = _lane_tile(alpha, 2) * acc_sc[...] + jnp.dot(p.astype(BF16), v, preferred_element_type=F32)
    m_sc[...] = m_new


def _attn_prompt_kernel(t, lamp_ref, q_ref, k_ref, v_ref, o_ref, q2_sc, m_sc, acc_sc):
    iq = pl.program_id(2)
    q2_sc[...] = _stack_sub_queries(q_ref[0])
    m_sc[...] = jnp.full_like(m_sc, NEG_BIG)
    acc_sc[...] = jnp.zeros_like(acc_sc)

    def scores(ik):
        k0 = pl.multiple_of(ik * t, t)
        s = lax.dot_general(q2_sc[...], k_ref[0, pl.ds(k0, t), :], (((1,), (1,)), ((), ())),
                            preferred_element_type=F32)
        return s, v_ref[0, pl.ds(k0, t), :]

    def causal(s):
        row = lax.broadcasted_iota(jnp.int32, s.shape, 0)
        col = lax.broadcasted_iota(jnp.int32, s.shape, 1)
        return jnp.where(col <= jnp.where(row >= t, row - t, row), s, NEG_BIG)

    def tiles(indices, diag_last):
        sv = [scores(i) for i in indices]
        for n, (s, v) in enumerate(sv):
            _flash_update(causal(s) if diag_last and n == len(sv) - 1 else s, v, m_sc, acc_sc)

    def group_body(i, c):
        tiles([ATTN_GROUP * i + n for n in range(ATTN_GROUP)], False)
        return c

    lax.fori_loop(0, iq // ATTN_GROUP, group_body, 0)

    for rest in range(ATTN_GROUP):
        @pl.when(iq % ATTN_GROUP == rest)
        def _():
            tiles([iq - rest + n for n in range(rest)] + [iq], True)

    lam = _lambda_value(lamp_ref)
    acc = acc_sc[...]
    o = acc[:, :ATTN_DV] / acc[:, ATTN_DV:]
    o_ref[0] = o[:t] - lam * o[t:]


def _attn_prompt(lamp, qb, kb, vb, t):
    b, lp, _ = qb.shape
    return pl.pallas_call(
        functools.partial(_attn_prompt_kernel, t),
        grid=(b, N_HEADS, lp // t),
        in_specs=[pl.BlockSpec(lamp.shape, lambda i, h, j: (0, 0)),
                  pl.BlockSpec((1, t, ATTN_DV), lambda i, h, j: (i, j, h)),
                  pl.BlockSpec((1, lp, ATTN_DV), lambda i, h, j: (i, 0, h)),
                  pl.BlockSpec((1, lp, 2 * ATTN_DV), lambda i, h, j: (i, 0, h))],
        out_specs=pl.BlockSpec((1, t, ATTN_DV), lambda i, h, j: (i, j, h)),
        out_shape=jax.ShapeDtypeStruct((b, lp, ATTN_WIDTH), F32),
        scratch_shapes=[pltpu.VMEM((2 * t, ATTN_DV), BF16),
                        pltpu.VMEM((2 * t, LANES), F32),
                        pltpu.VMEM((2 * t, 2 * ATTN_DV), F32)],
        compiler_params=_cparams(("parallel", "parallel", "parallel")),
        name="attn_prompt",
    )(lamp, qb, kb, vb)


def _attn_sample_kernel(n_pages, s_len, pt_ref, lamp_ref, q_ref, kn_ref, vn_ref, *refs):
    kt_refs = refs[:n_pages]
    v_refs = refs[n_pages:2 * n_pages]
    o_ref, q2_sc, m_sc, l_sc, acc_sc = refs[2 * n_pages:]
    j = pl.program_id(1)
    hr = 2 * s_len

    @pl.when(j == 0)
    def _():
        q = q_ref[0]
        lane = lax.broadcasted_iota(jnp.int32, q.shape, 1)
        zero = jnp.zeros_like(q)
        q2_sc[...] = jnp.concatenate(
            [jnp.where((lane >= lo) & (lane < lo + ATTN_DK), q, zero) for lo in range(0, ATTN_WIDTH, ATTN_DK)], axis=0)
        m_sc[...] = jnp.full_like(m_sc, NEG_BIG)
        l_sc[...] = jnp.zeros_like(l_sc)
        acc_sc[...] = jnp.zeros_like(acc_sc)

    def update(s, values):
        tk = s.shape[1]
        m_prev = m_sc[...]
        m_new = jnp.maximum(m_prev, jnp.max(s, axis=1, keepdims=True))
        p = jnp.exp2(s - _lane_tile(m_new, tk // LANES))
        alpha = jnp.exp2(m_prev - m_new)
        l_sc[...] = alpha * l_sc[...] + jnp.sum(p, axis=1, keepdims=True)
        pb = p.astype(BF16)
        for h in range(N_HEADS):
            rs = slice(h * hr, (h + 1) * hr)
            acc_sc[rs, :] = alpha[rs] * acc_sc[rs, :] + jnp.dot(pb[rs], values(h), preferred_element_type=F32)
        m_sc[...] = m_new

    kt = jnp.concatenate([r[0] for r in kt_refs], axis=1).astype(BF16)
    s = jnp.dot(q2_sc[...], kt, preferred_element_type=F32)
    update(s, lambda h: jnp.concatenate([r[0, pl.ds(h, LANES, stride=N_HEADS), :] for r in v_refs],
                                        axis=0).astype(BF16))

    @pl.when(j == pl.num_programs(1) - 1)
    def _():
        lam = _lambda_value(lamp_ref)
        pad_k = jnp.zeros((LANES - s_len, ATTN_WIDTH), BF16)
        pad_v = jnp.zeros((LANES - s_len, ATTN_DV), BF16)
        s_own = lax.dot_general(q2_sc[...], jnp.concatenate([kn_ref[0], pad_k], axis=0),
                                (((1,), (1,)), ((), ())), preferred_element_type=F32)
        row = lax.broadcasted_iota(jnp.int32, s_own.shape, 0)
        col = lax.broadcasted_iota(jnp.int32, s_own.shape, 1)
        s_own = jnp.where(col <= (row & (s_len - 1)), s_own, NEG_BIG)
        update(s_own, lambda h: jnp.concatenate([vn_ref[0, :, 2 * h * ATTN_DV:(2 * h + 1) * ATTN_DV], pad_v], axis=0))
        o = acc_sc[...] / l_sc[...]
        for h in range(N_HEADS):
            o_ref[0, :, h * ATTN_DV:(h + 1) * ATTN_DV] = (o[h * hr:h * hr + s_len]
                                                          - lam * o[h * hr + s_len:(h + 1) * hr])


def _attn_sample(lamp, qb, kb, vb, cache_kt, cache_v, page_table, n_pages):
    db, s_len, _ = qb.shape
    n_past_pages = page_table.shape[1]
    assert cache_kt.shape[2] == LANES and s_len & (s_len - 1) == 0 and n_past_pages % n_pages == 0
    steps = n_past_pages // n_pages
    tok = lambda w: pl.BlockSpec((1, s_len, w), lambda i, j, pt: (i, 0, 0))

    def page_spec(p):
        return pl.BlockSpec((1, ATTN_WIDTH, LANES), lambda i, j, pt: (pt[i, j * n_pages + p], 0, 0))

    rows = 2 * s_len * N_HEADS
    return pl.pallas_call(
        functools.partial(_attn_sample_kernel, n_pages, s_len),
        grid_spec=pltpu.PrefetchScalarGridSpec(
            num_scalar_prefetch=1,
            grid=(db, steps),
            in_specs=[pl.BlockSpec(lamp.shape, lambda i, j, pt: (0, 0)),
                      tok(ATTN_WIDTH), tok(ATTN_WIDTH), tok(2 * ATTN_WIDTH)]
                     + [page_spec(p) for p in range(n_pages)] * 2,
            out_specs=tok(ATTN_WIDTH),
            scratch_shapes=[pltpu.VMEM((rows, ATTN_WIDTH), BF16),
                            pltpu.VMEM((rows, LANES), F32),
                            pltpu.VMEM((rows, LANES), F32),
                            pltpu.VMEM((rows, ATTN_DV), F32)]),
        out_shape=jax.ShapeDtypeStruct((db, s_len, ATTN_WIDTH), F32),
        compiler_params=_cparams(("parallel", "arbitrary")),
        name="attn_sample",
    )(page_table, lamp, qb, kb, vb, *([cache_kt] * n_pages), *([cache_v] * n_pages))


def _finish_kernel(h_ref, ys_ref, o_ref, sg_ref, wout_ref, g2_ref, wq_ref, keys_ref,
                   h1_ref, n2_ref, st_ref):
    o = o_ref[0]
    parts = []
    for h in range(N_HEADS):
        oh = o[:, h * ATTN_DV:(h + 1) * ATTN_DV]
        parts.append(oh * lax.rsqrt(jnp.mean(oh * oh, axis=-1, keepdims=True) + EPS))
    on = jnp.concatenate(parts, axis=1) * sg_ref[...] * (1.0 - LAMBDA_INIT)
    mix = jnp.concatenate([ys_ref[0], on.astype(BF16)], axis=1)
    h1 = h_ref[0] + jnp.dot(mix, wout_ref[...], preferred_element_type=F32)
    h1_ref[0] = h1
    n2 = (h1 * lax.rsqrt(jnp.mean(h1 * h1, axis=-1, keepdims=True) + EPS) * g2_ref[...]).astype(BF16)
    n2_ref[0] = n2
    qt = lax.dot_general(wq_ref[...], n2, (((1,), (1,)), ((), ())), preferred_element_type=F32)
    for hp in range(2 * PEER_HEADS):
        st_ref[0, hp] = jnp.dot(keys_ref[hp], qt[hp * N_KEYS:(hp + 1) * N_KEYS].astype(BF16),
                                preferred_element_type=F32)


def _finish(h, ys, o, tm, sg, wout, g2, wq_t, keys):
    b, lp, _ = h.shape
    nblk = lp // tm
    tok = lambda w: pl.BlockSpec((1, tm, w), lambda i, j: (i, j, 0))
    full = lambda a: pl.BlockSpec(a.shape, lambda i, j: (0,) * a.ndim)
    return pl.pallas_call(
        _finish_kernel,
        grid=(b, nblk),
        in_specs=[tok(D_MODEL), tok(SSM_WIDTH), tok(ATTN_WIDTH), full(sg), full(wout), full(g2),
                  full(wq_t), full(keys)],
        out_specs=[tok(D_MODEL), tok(D_MODEL),
                   pl.BlockSpec((1, 2 * PEER_HEADS, N_KEYS, tm), lambda i, j: (i, 0, 0, j))],
        out_shape=[jax.ShapeDtypeStruct((b, lp, D_MODEL), F32),
                   jax.ShapeDtypeStruct((b, lp, D_MODEL), BF16),
                   jax.ShapeDtypeStruct((b, 2 * PEER_HEADS, N_KEYS, lp), F32)],
        compiler_params=_cparams(("parallel", "parallel")),
        name="finish",
    )(h, ys, o, sg, wout, g2, wq_t, keys)


def _cmpx(v, i, j):
    hi = jnp.maximum(v[i], v[j])
    lo = jnp.minimum(v[i], v[j])
    v[i], v[j] = hi, lo


def _bitonic_merge_desc(v):
    n = len(v)
    v = list(v)
    d = n // 2
    while d >= 1:
        for i in range(n):
            if (i % (2 * d)) < d:
                _cmpx(v, i, i + d)
        d //= 2
    return v


def _sort_desc(v):
    n = len(v)
    if n == 1:
        return list(v)
    a = _sort_desc(v[:n // 2])
    b = _sort_desc(v[n // 2:])
    return _bitonic_merge_desc(a + b[::-1])


def _merge_top(t, s):
    n = len(t)
    m = list(t)
    for r in range(n):
        q = n - 1 - r
        if q < len(s):
            m[r] = jnp.maximum(t[r], s[q])
    return _bitonic_merge_desc(m)


def _pack_pair(lo, hi):
    lo_bits = lax.bitcast_convert_type(lo.astype(BF16).astype(F32), jnp.uint32)
    hi_bits = lax.bitcast_convert_type(hi.astype(BF16).astype(F32), jnp.uint32)
    return lax.bitcast_convert_type(hi_bits | (lo_bits >> 16), F32)


def _twice_bf16(x):
    return _pack_pair(x, x)


def _peer_select_kernel(s_ref, cnt_ref, w1_ref, rank_ref, e2_ref):
    tb = s_ref.shape[-1]
    sub = lax.broadcasted_iota(jnp.int32, (SUBLANES, LANES), 0)
    k = PEER_TOPK

    def lane_group(ln):
        tops = []
        for hp in range(2 * PEER_HEADS):
            col = _sort_desc([s_ref[0, hp, r * SUBLANES:(r + 1) * SUBLANES, ln] for r in range(N_KEYS // SUBLANES)])
            for shift in (4, 2, 1):
                other = [pltpu.roll(c, shift, 0) for c in col]
                col = _merge_top(col, other)
            tops.append(col)

        def pack(lists):
            out = []
            for r in range(k):
                x = lists[0][r]
                for h in range(1, PEER_HEADS):
                    x = jnp.where(sub == h, lists[h][r], x)
                out.append(x)
            return out

        a = pack([tops[2 * h] for h in range(PEER_HEADS)])
        b = pack([tops[2 * h + 1] for h in range(PEER_HEADS)])
        t = [a[0] + b[q] for q in range(k)]
        for i in range(1, k // 2):
            t = _merge_top(t, [a[i] + b[q] for q in range(k // (i + 1))])
        t = _merge_top(t, [a[i] + b[0] for i in range(k // 2, k)])
        theta = t[k - 1]
        z = jnp.ones_like(theta)
        for r in range(1, k):
            z = z + jnp.exp(t[r] - t[0])
        inv_z = 1.0 / z
        counts = []
        for r in range(k):
            cnt = jnp.zeros_like(theta)
            for q in range(k // (r + 1)):
                cnt = cnt + jnp.where(a[r] + b[q] >= theta, 1.0, 0.0)
            counts.append(cnt)

        for h in range(PEER_HEADS):
            bc = lambda x: jnp.broadcast_to(x[h:h + 1, :], (SUBLANES, LANES))
            a_h = tops[2 * h]
            b_h = tops[2 * h + 1]
            cnt_h = [bc(x) for x in counts]
            inv_z_h = bc(inv_z)
            for r in range(N_KEYS // SUBLANES):
                rs = slice(r * SUBLANES, (r + 1) * SUBLANES)
                s1 = s_ref[0, 2 * h, rs, ln]
                cnt = jnp.zeros_like(s1)
                for q in range(k):
                    cnt = jnp.where(s1 == a_h[q], cnt_h[q], cnt)
                cnt_ref[0, h, rs, ln] = _twice_bf16(cnt)
                w1_ref[0, h, rs, ln] = _twice_bf16(jnp.exp(s1 - a_h[0]) * inv_z_h)

            def second_key(start):
                s2 = s_ref[0, 2 * h + 1, pl.ds(start, SUBLANES, stride=2), ln]
                rank = jnp.full_like(s2, float(N_KEYS - 1))
                for q in range(k):
                    rank = jnp.where(s2 == b_h[q], float(q), rank)
                return rank, jnp.exp(s2 - b_h[0])

            for g in range(N_KEYS // BF16_ROWS):
                rank_even, e2_even = second_key(g * BF16_ROWS)
                rank_odd, e2_odd = second_key(g * BF16_ROWS + 1)
                ws = slice(g * SUBLANES, (g + 1) * SUBLANES)
                rank_ref[0, h, ws, ln] = _pack_pair(rank_even, rank_odd)
                e2_ref[0, h, ws, ln] = _pack_pair(e2_even, e2_odd)

    for g in range(tb // LANES):
        lane_group(slice(g * LANES, (g + 1) * LANES))


def _peer_select(st):
    tb = LANES
    b, _, _, lp = st.shape
    out = lambda rows: jax.ShapeDtypeStruct((b, PEER_HEADS, rows, lp), F32)
    spec = lambda rows: pl.BlockSpec((1, PEER_HEADS, rows, tb), lambda i, j: (i, 0, 0, j))
    return pl.pallas_call(
        _peer_select_kernel,
        grid=(b, lp // tb),
        in_specs=[pl.BlockSpec((1, 2 * PEER_HEADS, N_KEYS, tb), lambda i, j: (i, 0, 0, j))],
        out_specs=[spec(N_KEYS), spec(N_KEYS), spec(N_KEYS // 2), spec(N_KEYS // 2)],
        out_shape=[out(N_KEYS), out(N_KEYS), out(N_KEYS // 2), out(N_KEYS // 2)],
        compiler_params=_cparams(("parallel", "parallel")),
        name="peer_select",
    )(st)


def _peer_dense_kernel(n2_ref, u_ref, vt_ref, rank_ref, e2_ref, cnta_ref, w1a_ref, cntb_ref, w1b_ref, h1_ref,
                       out_ref, a0_sc, a1_sc, w0_sc, w1_sc, acc_sc):
    k = pl.program_id(2)
    tb = n2_ref.shape[1]

    @pl.when(k == 0)
    def _():
        a1_sc[...] = jnp.zeros_like(a1_sc)
        w0_sc[...] = jnp.zeros_like(w0_sc)
        acc_sc[...] = jnp.zeros_like(acc_sc)

    n2 = n2_ref[0]

    def activations(half, a_sc):
        rows = slice(half * PEER_CHUNK, (half + 1) * PEER_CHUNK)
        a = lax.dot_general(u_ref[rows, :], n2, (((1,), (1,)), ((), ())), preferred_element_type=F32)
        a_sc[...] = pltpu.bitcast(_gelu(a).astype(BF16), F32)

    def output(half, w_sc):
        cols = slice(half * PEER_CHUNK, (half + 1) * PEER_CHUNK)
        acc_sc[...] += jnp.dot(vt_ref[:, cols], pltpu.bitcast(w_sc[...], BF16), preferred_element_type=F32)

    def gate(cnt_ref, w1_ref, a_sc, w_sc):
        for pair in range(PEER_CHUNK // PEER_SUB):
            gate_rows(cnt_ref, w1_ref, a_sc, w_sc, pair * (PEER_SUB // N_KEYS))

    def gate_rows(cnt_ref, w1_ref, a_sc, w_sc, il0):
        for cc in range(tb // LANES):
            ln = slice(cc * LANES, (cc + 1) * LANES)
            bc = lambda ref, h, il: pltpu.bitcast(jnp.broadcast_to(ref[0, h, il:il + 1, ln], (SUBLANES, LANES)), BF16)
            n_jp = N_KEYS // BF16_ROWS
            zero = jnp.zeros((BF16_ROWS, LANES), BF16)
            g = [[zero] * n_jp for _ in range(2)]
            for h in range(PEER_HEADS):
                cnt = [bc(cnt_ref, h, il0 + d) for d in range(2)]
                w1 = [bc(w1_ref, h, il0 + d) for d in range(2)]
                for jp in range(n_jp):
                    ws = slice(jp * SUBLANES, (jp + 1) * SUBLANES)
                    rank = pltpu.bitcast(rank_ref[0, h, ws, ln], BF16)
                    e2 = pltpu.bitcast(e2_ref[0, h, ws, ln], BF16)
                    for d in range(2):
                        g[d][jp] = g[d][jp] + jnp.where(rank < cnt[d], e2 * w1[d], zero)
            for jp in range(n_jp):
                for d in range(2):
                    r0 = (il0 + d) * N_KEYS + jp * BF16_ROWS
                    w = g[d][jp] * pltpu.bitcast(a_sc[r0 // 2:r0 // 2 + SUBLANES, ln], BF16)
                    w_sc[r0 // 2:r0 // 2 + SUBLANES, ln] = pltpu.bitcast(w, F32)

    activations(0, a0_sc)
    gate(cnta_ref, w1a_ref, a1_sc, w1_sc)
    output(0, w0_sc)
    activations(1, a1_sc)
    gate(cntb_ref, w1b_ref, a0_sc, w0_sc)
    output(1, w1_sc)

    @pl.when(k == pl.num_programs(2) - 1)
    def _():
        out_ref[0] = h1_ref[0] + acc_sc[...].T


def _peer_dense(n2, u_bf, vt_bf, rank, e2, cnt, w1, h1, tb):
    b, lp, _ = n2.shape
    n_chunks = N_EXPERTS // PEER_CHUNK
    steps = n_chunks // 2 + 1
    rows = PEER_CHUNK // N_KEYS
    keyed = pl.BlockSpec((1, PEER_HEADS, N_KEYS // 2, tb), lambda i, j, k: (i, 0, 0, j))
    rows_a = pl.BlockSpec((1, PEER_HEADS, rows, tb), lambda i, j, k: (i, 0, jnp.maximum(2 * k - 1, 0), j))
    rows_b = pl.BlockSpec((1, PEER_HEADS, rows, tb), lambda i, j, k: (i, 0, jnp.minimum(2 * k, n_chunks - 1), j))
    return pl.pallas_call(
        _peer_dense_kernel,
        grid=(b, lp // tb, steps),
        in_specs=[pl.BlockSpec((1, tb, D_MODEL), lambda i, j, k: (i, j, 0)),
                  pl.BlockSpec((2 * PEER_CHUNK, D_MODEL), lambda i, j, k: (jnp.minimum(k, steps - 2), 0)),
                  pl.BlockSpec((D_MODEL, 2 * PEER_CHUNK), lambda i, j, k: (0, jnp.maximum(k - 1, 0))),
                  keyed, keyed, rows_a, rows_a, rows_b, rows_b,
                  pl.BlockSpec((1, tb, D_MODEL), lambda i, j, k: (i, j, 0))],
        out_specs=pl.BlockSpec((1, tb, D_MODEL), lambda i, j, k: (i, j, 0)),
        out_shape=jax.ShapeDtypeStruct((b, lp, D_MODEL), F32),
        scratch_shapes=[pltpu.VMEM((PEER_CHUNK // 2, tb), F32), pltpu.VMEM((PEER_CHUNK // 2, tb), F32),
                        pltpu.VMEM((PEER_CHUNK // 2, tb), F32), pltpu.VMEM((PEER_CHUNK // 2, tb), F32),
                        pltpu.VMEM((D_MODEL, tb), F32)],
        compiler_params=_cparams(("parallel", "parallel", "arbitrary")),
        name="peer_dense",
    )(n2, u_bf, vt_bf, rank, e2, cnt, w1, cnt, w1, h1)


def _rope_tables(pos):
    half = ATTN_DK // 2
    inv_freq = ROPE_THETA ** (-jnp.arange(half, dtype=F32) * 2.0 / ATTN_DK)
    ang = pos[:, None] * inv_freq[None, :]
    cos = jnp.cos(ang)
    sin = jnp.sin(ang)
    return (jnp.concatenate([cos, cos, cos, cos], axis=1),
            jnp.concatenate([-sin, sin, -sin, sin], axis=1))


def _round_up(x, m):
    return (x + m - 1) // m * m


def _layer(h, l_real, pos, weights, tm, tb, attend):
    (g1, w_in, qg, kg, gm, ssm_c, sg, wout, g2, wq_t, keys, u_bf, vt_bf) = weights
    cos_t, sin_t = _rope_tables(pos)
    u, k, v, qb, kb, vb = _project(h, l_real, tm, g1, w_in, qg, kg, cos_t, sin_t, gm)
    o, ys, st = attend(u, qb, kb, vb, ssm_c)
    h1, n2, sc = _finish(h, ys, o, tb, sg, wout, g2, wq_t, keys)
    cnt, w1, rank, e2 = _peer_select(sc)
    out = _peer_dense(n2, u_bf, vt_bf, rank, e2, cnt, w1, h1, tb)
    return out, k, v, st


def kernel(x_prompt, x_sample, cache_k, cache_v, state_ssm_re, state_ssm_im, page_table, meta_tokens, norm1_g, w_in, q_norm_g, k_norm_g, lambda_q1, lambda_k1, lambda_q2, lambda_k2, subln_g, ssm_A_re, ssm_A_im, ssm_log_dt, ssm_B_re, ssm_B_im, ssm_C_re, ssm_C_im, ssm_D, w_glu, b_glu, w_out, norm2_g, peer_w_q, peer_keys, peer_u, peer_v):
    assert w_in.shape[0] == 1, "single-layer trunk"
    bp, seq, _ = x_prompt.shape
    db, ds, _ = x_sample.shape
    lp_real = seq + N_META
    assert lp_real % SUBLANES == 0 and ds == SUBLANES
    tm_p, t_attn, tb = 768, 512, 512
    lp = _round_up(lp_real, math.lcm(tm_p, t_attn, tb))
    past = page_table.shape[1] * cache_k.shape[2]

    gm = jnp.kron(jnp.eye(ATTN_WIDTH // ATTN_DK, dtype=F32),
                  jnp.full((ATTN_DK, ATTN_DK), 1.0 / ATTN_DK, F32)).astype(BF16)
    lamp = jnp.stack([lambda_q1[0], lambda_k1[0], lambda_q2[0], lambda_k2[0]]).astype(F32)
    ssm_c = _ssm_consts(ssm_A_re[0], ssm_A_im[0], ssm_log_dt[0], ssm_B_re[0], ssm_B_im[0],
                        ssm_C_re[0], ssm_C_im[0], ssm_D[0], w_glu[0], b_glu[0])
    weights = (norm1_g[0].astype(F32)[None, :], w_in[0].astype(BF16),
               jnp.tile(q_norm_g[0].astype(F32), ATTN_WIDTH // ATTN_DK)[None, :],
               jnp.tile(k_norm_g[0].astype(F32), ATTN_WIDTH // ATTN_DK)[None, :],
               gm, ssm_c,
               jnp.tile(subln_g[0].astype(F32), N_HEADS)[None, :], w_out[0].astype(BF16),
               norm2_g[0].astype(F32)[None, :], peer_w_q[0].T.astype(BF16),
               peer_keys[0].reshape(2 * PEER_HEADS, N_KEYS, N_KEYS).astype(BF16),
               peer_u[0].astype(BF16), peer_v[0].T.astype(BF16))

    hp = jnp.concatenate([jnp.broadcast_to(meta_tokens.astype(F32)[None], (bp, N_META, D_MODEL)),
                          x_prompt.astype(F32),
                          jnp.zeros((bp, lp - lp_real, D_MODEL), F32)], axis=1)

    def attend_prompt(u, qb, kb, vb, consts):
        ys, st = _ssm(u, tm_p, lp_real - 1, consts)
        return _attn_prompt(lamp, qb, kb, vb, t_attn), ys, st

    hp_out, k_p, v_p, st_p = _layer(hp, lp_real, jnp.arange(lp, dtype=F32), weights, tm_p, tb, attend_prompt)

    n_tok = db * ds
    hs = x_sample.astype(F32).reshape(1, n_tok, D_MODEL)
    h0 = jnp.concatenate([state_ssm_re[0].reshape(db, SSM_FLAT), state_ssm_im[0].reshape(db, SSM_FLAT)],
                         axis=1).astype(F32)[None]
    n_pool, page = cache_k.shape[1], cache_k.shape[2]
    ckt = jnp.transpose(cache_k[0], (0, 2, 3, 4, 1)).reshape(n_pool, ATTN_WIDTH, page)
    cv = cache_v[0].reshape(n_pool, page * N_HEADS, ATTN_DV)

    def attend_sample(u, qb, kb, vb, consts):
        ys, st = _ssm(u, 256, None, consts, h0=h0)
        seqs = lambda a: a.reshape(db, ds, a.shape[-1])
        o = _attn_sample(lamp, seqs(qb), seqs(kb), seqs(vb), ckt, cv, page_table, 16)
        return o.reshape(1, n_tok, ATTN_WIDTH), ys, st

    pos_s = jnp.tile(past + jnp.arange(ds, dtype=F32), db)
    hs_out, k_s, v_s, st_s = _layer(hs, n_tok, pos_s, weights, 512, tb, attend_sample)

    y_prompt = hp_out[:, N_META:lp_real].astype(x_prompt.dtype)
    y_sample = hs_out.reshape(db, ds, D_MODEL).astype(x_sample.dtype)
    row = (lp_real - 1) % SUBLANES
    kd, vd, sd = cache_k.dtype, cache_v.dtype, state_ssm_re.dtype
    return (y_prompt, y_sample,
            k_p.reshape(1, bp, lp_real, N_HEADS, 2, ATTN_DK).astype(kd),
            v_p.reshape(1, bp, lp_real, N_HEADS, ATTN_DV).astype(vd),
            st_p[:, row, :SSM_FLAT].reshape(1, bp, SSM_GROUPS, SSM_STATE).astype(sd),
            st_p[:, row, SSM_FLAT:].reshape(1, bp, SSM_GROUPS, SSM_STATE).astype(state_ssm_im.dtype),
            k_s.reshape(1, db, ds, N_HEADS, 2, ATTN_DK).astype(kd),
            v_s.reshape(1, db, ds, N_HEADS, ATTN_DV).astype(vd),
            st_s[0, ds - 1::ds, :SSM_FLAT].reshape(1, db, SSM_GROUPS, SSM_STATE).astype(sd),
            st_s[0, ds - 1::ds, SSM_FLAT:].reshape(1, db, SSM_GROUPS, SSM_STATE).astype(state_ssm_im.dtype))
```

```python
import functools
import math

import jax
import jax.numpy as jnp
from jax import lax
from jax.experimental import pallas as pl
from jax.experimental.pallas import tpu as pltpu

F32 = jnp.float32
BF16 = jnp.bfloat16

D_MODEL = 1024
N_META = 16
SSM_WIDTH = 512
SSM_GROUP = 16
SSM_GROUPS = 32
SSM_STATE = 64
SSM_FLAT = SSM_GROUPS * SSM_STATE
ATTN_WIDTH = 512
ATTN_DV = 128
N_HEADS = 4
ATTN_DK = 64
IN_WIDTH = 2048
ROPE_THETA = 10000.0
N_KEYS = 128
N_EXPERTS = N_KEYS * N_KEYS
PEER_HEADS = 8
PEER_TOPK = 16
PEER_SUB = 2 * N_KEYS
EPS = 1e-6
NEG_BIG = -1e30
POS_BIG = 3e38
LAMBDA_INIT = 0.8 - 0.6 * math.exp(-0.3 * 0)
Q_SCALE = ATTN_DK ** -0.5 * math.log2(math.e)
ATTN_GROUP = 4

SUBLANES = 8
LANES = 128
BF16_ROWS = 2 * SUBLANES
VMEM_LIMIT = 56 * 1024 * 1024


def _cparams(sem):
    return pltpu.CompilerParams(dimension_semantics=sem, vmem_limit_bytes=VMEM_LIMIT)


GELU_C1 = 0.7978845608028654
GELU_C2 = GELU_C1 * 0.044715


def _gelu(x):
    return x * (0.5 + 0.5 * jnp.tanh(x * (GELU_C1 + GELU_C2 * (x * x))))


def _proj_kernel(x_ref, g1_ref, w_ref, qg_ref, kg_ref, cos_ref, sin_ref, gm_ref,
                 u_ref, k_ref, v_ref, qb_ref, kb_ref, vb_ref):
    x = x_ref[0]
    tm = x.shape[0]
    ms = jnp.mean(x * x, axis=-1, keepdims=True)
    n = (x * lax.rsqrt(ms + EPS) * g1_ref[...]).astype(BF16)
    z = jnp.dot(n, w_ref[...], preferred_element_type=F32)
    u_ref[0] = z[:, :SSM_WIDTH]
    v = z[:, SSM_WIDTH + 2 * ATTN_WIDTH:]
    v_ref[0] = v
    vb = v.astype(BF16)
    ones = jnp.ones((tm, ATTN_DV), BF16)
    vb_ref[0] = jnp.concatenate([t for h in range(N_HEADS) for t in (vb[:, h * ATTN_DV:(h + 1) * ATTN_DV], ones)], axis=1)

    cos = jnp.concatenate([cos_ref[...]] * 4, axis=1)
    sin = jnp.concatenate([sin_ref[...]] * 4, axis=1)
    lane = lax.broadcasted_iota(jnp.int32, (tm, ATTN_WIDTH), 1)
    first_half = (lane & (ATTN_DK - 1)) < (ATTN_DK // 2)
    gm = gm_ref[...]

    def norm_rope(t, g):
        sq = t * t
        hi = sq.astype(BF16)
        lo = (sq - hi.astype(F32)).astype(BF16)
        ms64 = (jnp.dot(hi, gm, preferred_element_type=F32)
                + jnp.dot(lo, gm, preferred_element_type=F32))
        tn = t * lax.rsqrt(ms64 + EPS) * g
        swapped = jnp.where(first_half,
                            pltpu.roll(tn, ATTN_WIDTH - ATTN_DK // 2, 1),
                            pltpu.roll(tn, ATTN_DK // 2, 1))
        return tn * cos + swapped * sin

    q = norm_rope(z[:, SSM_WIDTH:SSM_WIDTH + ATTN_WIDTH], qg_ref[...])
    k = norm_rope(z[:, SSM_WIDTH + ATTN_WIDTH:SSM_WIDTH + 2 * ATTN_WIDTH], kg_ref[...])
    k_ref[0] = k
    kb_ref[0] = k.astype(BF16)
    qb_ref[0] = (q * Q_SCALE).astype(BF16)


def _project(h, l_out, tm, g1, w_bf, qg, kg, cos_t, sin_t, gm):
    b, lp, _ = h.shape
    nblk = lp // tm
    tok = lambda w: pl.BlockSpec((1, tm, w), lambda i, j: (i, j, 0))
    full = lambda a: pl.BlockSpec(a.shape, lambda i, j: (0,) * a.ndim)
    return pl.pallas_call(
        _proj_kernel,
        grid=(b, nblk),
        in_specs=[tok(D_MODEL), full(g1), full(w_bf), full(qg), full(kg),
                  pl.BlockSpec((tm, LANES), lambda i, j: (j, 0)),
                  pl.BlockSpec((tm, LANES), lambda i, j: (j, 0)), full(gm)],
        out_specs=[tok(SSM_WIDTH), tok(ATTN_WIDTH), tok(ATTN_WIDTH),
                   tok(ATTN_WIDTH), tok(ATTN_WIDTH), tok(2 * ATTN_WIDTH)],
        out_shape=[jax.ShapeDtypeStruct((b, lp, SSM_WIDTH), F32),
                   jax.ShapeDtypeStruct((b, l_out, ATTN_WIDTH), F32),
                   jax.ShapeDtypeStruct((b, l_out, ATTN_WIDTH), F32),
                   jax.ShapeDtypeStruct((b, lp, ATTN_WIDTH), BF16),
                   jax.ShapeDtypeStruct((b, lp, ATTN_WIDTH), BF16),
                   jax.ShapeDtypeStruct((b, lp, 2 * ATTN_WIDTH), BF16)],
        compiler_params=_cparams(("parallel", "parallel")),
        name="project",
    )(h, g1, w_bf, qg, kg, cos_t, sin_t, gm)


SSM_CHUNK = 512


def _ssm_kernel(sequential, state_row, state_blk, *refs):
    if sequential:
        (u_ref, bbd_ref, cbd_ref, kc_ref, pc_ref, d_ref, wg_ref, bg_ref,
         y_ref, st_ref, x_sc, carry_sc) = refs
        h0_ref = None
    else:
        (u_ref, bbd_ref, cbd_ref, kc_ref, pc_ref, d_ref, wg_ref, bg_ref, h0_ref,
         y_ref, st_ref, x_sc) = refs
        carry_sc = None
    j = pl.program_id(1)
    u = u_ref[0]
    tm = u.shape[0]
    ub = u.astype(BF16)
    hw, hf = SSM_WIDTH // 2, SSM_FLAT // 2
    for half in range(2):
        xh = jnp.dot(ub[:, half * hw:(half + 1) * hw], bbd_ref[half], preferred_element_type=F32)
        x_sc[:, half * hf:(half + 1) * hf] = xh[:, :hf]
        x_sc[:, SSM_FLAT + half * hf:SSM_FLAT + (half + 1) * hf] = xh[:, hf:]

    if sequential:
        @pl.when(j == 0)
        def _():
            carry_sc[...] = jnp.zeros_like(carry_sc)

    def tile_body(i, carry):
        r0 = pl.multiple_of(i * SUBLANES, SUBLANES)
        for c in range(0, SSM_FLAT, SSM_CHUNK):
            re = slice(c, c + SSM_CHUNK)
            im = slice(SSM_FLAT + c, SSM_FLAT + c + SSM_CHUNK)
            xr = x_sc[pl.ds(r0, SUBLANES), re]
            xi = x_sc[pl.ds(r0, SUBLANES), im]
            for di, d in enumerate((1, 2, 4)):
                cr = kc_ref[di, 0, :, re]
                ci = kc_ref[di, 1, :, re]
                rr = pltpu.roll(xr, d, 0)
                ri = pltpu.roll(xi, d, 0)
                xr, xi = xr + cr * rr - ci * ri, xi + cr * ri + ci * rr
            if sequential:
                car_r = carry_sc[:, re]
                car_i = carry_sc[:, im]
            else:
                car_r = h0_ref[0, pl.ds(r0, SUBLANES), re]
                car_i = h0_ref[0, pl.ds(r0, SUBLANES), im]
            pr = pc_ref[0, :, re]
            pi = pc_ref[1, :, re]
            xr, xi = xr + pr * car_r - pi * car_i, xi + pr * car_i + pi * car_r
            x_sc[pl.ds(r0, SUBLANES), re] = xr
            x_sc[pl.ds(r0, SUBLANES), im] = xi
            if sequential:
                carry_sc[:, re] = jnp.broadcast_to(xr[SUBLANES - 1:, :], (SUBLANES, SSM_CHUNK))
                carry_sc[:, im] = jnp.broadcast_to(xi[SUBLANES - 1:, :], (SUBLANES, SSM_CHUNK))
        return carry

    lax.fori_loop(0, tm // SUBLANES, tile_body, 0)

    if sequential:
        @pl.when(j == state_blk)
        def _():
            st_ref[0] = x_sc[state_row:state_row + SUBLANES, :]
    else:
        st_ref[0] = x_sc[...]

    ys = []
    for half in range(2):
        x_re = x_sc[:, half * hf:(half + 1) * hf].astype(BF16)
        x_im = x_sc[:, SSM_FLAT + half * hf:SSM_FLAT + (half + 1) * hf].astype(BF16)
        ys.append(jnp.dot(x_re, cbd_ref[half, :hf], preferred_element_type=F32)
                  + jnp.dot(x_im, cbd_ref[half, hf:], preferred_element_type=F32))
    y = jnp.concatenate(ys, axis=1) + d_ref[...] * u
    y = _gelu(y)
    gate = jnp.dot(y.astype(BF16), wg_ref[...], preferred_element_type=F32) + bg_ref[...]
    y_ref[0] = (y * (1.0 / (1.0 + jnp.exp(-gate)))).astype(BF16)


def _ssm(u, tm, last_token, consts, h0=None):
    bbd, cbd, kc, pc, dsk, wg, bg = consts
    b, lp, _ = u.shape
    nblk = lp // tm
    sequential = h0 is None
    full = lambda a: pl.BlockSpec(a.shape, lambda i, j: (0,) * a.ndim)
    in_specs = [pl.BlockSpec((1, tm, SSM_WIDTH), lambda i, j: (i, j, 0)),
                full(bbd), full(cbd), full(kc), full(pc), full(dsk), full(wg), full(bg)]
    args = [u, bbd, cbd, kc, pc, dsk, wg, bg]
    scratch = [pltpu.VMEM((tm, 2 * SSM_FLAT), F32)]
    if sequential:
        state_blk = last_token // tm
        state_row = (last_token % tm) // SUBLANES * SUBLANES
        st_spec = pl.BlockSpec((1, SUBLANES, 2 * SSM_FLAT), lambda i, j: (i, 0, 0))
        st_shape = jax.ShapeDtypeStruct((b, SUBLANES, 2 * SSM_FLAT), F32)
        scratch.append(pltpu.VMEM((SUBLANES, 2 * SSM_FLAT), F32))
        sem = ("parallel", "arbitrary")
    else:
        state_blk = state_row = 0
        in_specs.append(pl.BlockSpec((1, tm, 2 * SSM_FLAT), lambda i, j: (i, j, 0)))
        args.append(jnp.repeat(h0, SUBLANES, axis=1))
        st_spec = pl.BlockSpec((1, tm, 2 * SSM_FLAT), lambda i, j: (i, j, 0))
        st_shape = jax.ShapeDtypeStruct((b, lp, 2 * SSM_FLAT), F32)
        sem = ("parallel", "parallel")
    return pl.pallas_call(
        functools.partial(_ssm_kernel, sequential, state_row, state_blk),
        grid=(b, nblk),
        in_specs=in_specs,
        out_specs=[pl.BlockSpec((1, tm, SSM_WIDTH), lambda i, j: (i, j, 0)), st_spec],
        out_shape=[jax.ShapeDtypeStruct((b, lp, SSM_WIDTH), BF16), st_shape],
        scratch_shapes=scratch,
        compiler_params=_cparams(sem),
        name="s5_prompt" if sequential else "s5_sample",
    )(*args)


def _ssm_consts(a_re, a_im, log_dt, b_re, b_im, c_re, c_im, d_skip, w_glu, b_glu):
    dt = jnp.exp(log_dt.astype(F32))[:, None]
    a_re = a_re.astype(F32)
    a_im = a_im.astype(F32)
    mag = jnp.exp(dt * a_re)
    ab_re = mag * jnp.cos(dt * a_im)
    ab_im = mag * jnp.sin(dt * a_im)
    den = a_re * a_re + a_im * a_im
    z_re = ((ab_re - 1.0) * a_re + ab_im * a_im) / den
    z_im = (ab_im * a_re - (ab_re - 1.0) * a_im) / den
    b_re = b_re.astype(F32)
    b_im = b_im.astype(F32)
    bb_re = z_re[..., None] * b_re - z_im[..., None] * b_im
    bb_im = z_re[..., None] * b_im + z_im[..., None] * b_re
    eye = jnp.eye(SSM_GROUPS, dtype=F32)
    bbd = jnp.concatenate(
        [jnp.einsum('gnc,gh->gchn', bb_re, eye).reshape(SSM_WIDTH, SSM_FLAT),
         jnp.einsum('gnc,gh->gchn', bb_im, eye).reshape(SSM_WIDTH, SSM_FLAT)], axis=1).astype(BF16)
    cbd = jnp.concatenate(
        [jnp.einsum('gcn,gh->gnhc', c_re.astype(F32), eye).reshape(SSM_FLAT, SSM_WIDTH),
         -jnp.einsum('gcn,gh->gnhc', c_im.astype(F32), eye).reshape(SSM_FLAT, SSM_WIDTH)], axis=0).astype(BF16)
    hw, hf = SSM_WIDTH // 2, SSM_FLAT // 2
    half_b = lambda k: jnp.concatenate([bbd[k * hw:(k + 1) * hw, k * hf:(k + 1) * hf],
                                        bbd[k * hw:(k + 1) * hw, SSM_FLAT + k * hf:SSM_FLAT + (k + 1) * hf]], axis=1)
    half_c = lambda k: jnp.concatenate([cbd[k * hf:(k + 1) * hf, k * hw:(k + 1) * hw],
                                        cbd[SSM_FLAT + k * hf:SSM_FLAT + (k + 1) * hf, k * hw:(k + 1) * hw]], axis=0)
    bbd = jnp.stack([half_b(0), half_b(1)])
    cbd = jnp.stack([half_c(0), half_c(1)])

    def power(p):
        m = jnp.exp(p * dt * a_re)
        return (m * jnp.cos(p * dt * a_im)).reshape(-1), (m * jnp.sin(p * dt * a_im)).reshape(-1)

    row = jnp.arange(SUBLANES)[:, None]
    kc = []
    for d in (1, 2, 4):
        pr, pi = power(float(d))
        kc.append(jnp.stack([jnp.where(row >= d, pr[None, :], 0.0), jnp.where(row >= d, pi[None, :], 0.0)]))
    kc = jnp.stack(kc)
    rows = [power(float(s + 1)) for s in range(SUBLANES)]
    pc = jnp.stack([jnp.stack([r[0] for r in rows]), jnp.stack([r[1] for r in rows])])
    return (bbd, cbd, kc, pc, d_skip.astype(F32)[None, :], w_glu.astype(BF16), b_glu.astype(F32)[None, :])


def _lambda_value(lamp_ref):
    lp = lamp_ref[...]
    s1 = jnp.sum(lp[0:1] * lp[1:2], axis=1, keepdims=True)
    s2 = jnp.sum(lp[2:3] * lp[3:4], axis=1, keepdims=True)
    return jnp.exp(s1) - jnp.exp(s2) + LAMBDA_INIT


def _stack_sub_queries(q):
    lane = lax.broadcasted_iota(jnp.int32, q.shape, 1)
    zero = jnp.zeros_like(q)
    return jnp.concatenate([jnp.where(lane < ATTN_DK, q, zero), jnp.where(lane >= ATTN_DK, q, zero)], axis=0)


def _lane_tile(x, n):
    return x if n == 1 else jnp.concatenate([x] * n, axis=1)


def _flash_update(s, v, m_sc, acc_sc):
    tk = s.shape[1]
    m_prev = m_sc[...]
    m_new = jnp.maximum(m_prev, jnp.max(s, axis=1, keepdims=True))
    p = jnp.exp2(s - _lane_tile(m_new, tk // LANES))
    alpha = jnp.exp2(m_prev - m_new)
    acc_sc[...] = _lane_tile(alpha, 2) * acc_sc[...] + jnp.dot(p.astype(BF16), v, preferred_element_type=F32)
    m_sc[...] = m_new


def _attn_prompt_kernel(t, lamp_ref, q_ref, k_ref, v_ref, o_ref, q2_sc, m_sc, acc_sc):
    iq = pl.program_id(2)
    q2_sc[...] = _stack_sub_queries(q_ref[0])
    m_sc[...] = jnp.full_like(m_sc, NEG_BIG)
    acc_sc[...] = jnp.zeros_like(acc_sc)

    def scores(ik):
        k0 = pl.multiple_of(ik * t, t)
        s = lax.dot_general(q2_sc[...], k_ref[0, pl.ds(k0, t), :], (((1,), (1,)), ((), ())),
                            preferred_element_type=F32)
        return s, v_ref[0, pl.ds(k0, t), :]

    def causal(s):
        row = lax.broadcasted_iota(jnp.int32, s.shape, 0)
        col = lax.broadcasted_iota(jnp.int32, s.shape, 1)
        return jnp.where(col <= jnp.where(row >= t, row - t, row), s, NEG_BIG)

    def tiles(indices, diag_last):
        sv = [scores(i) for i in indices]
        for n, (s, v) in enumerate(sv):
            _flash_update(causal(s) if diag_last and n == len(sv) - 1 else s, v, m_sc, acc_sc)

    def group_body(i, c):
        tiles([ATTN_GROUP * i + n for n in range(ATTN_GROUP)], False)
        return c

    lax.fori_loop(0, iq // ATTN_GROUP, group_body, 0)

    for rest in range(ATTN_GROUP):
        @pl.when(iq % ATTN_GROUP == rest)
        def _():
            tiles([iq - rest + n for n in range(rest)] + [iq], True)

    lam = _lambda_value(lamp_ref)
    acc = acc_sc[...]
    o = acc[:, :ATTN_DV] / acc[:, ATTN_DV:]
    o_ref[0] = o[:t] - lam * o[t:]


def _attn_prompt(lamp, qb, kb, vb, t):
    b, lp, _ = qb.shape
    return pl.pallas_call(
        functools.partial(_attn_prompt_kernel, t),
        grid=(b, N_HEADS, lp // t),
        in_specs=[pl.BlockSpec(lamp.shape, lambda i, h, j: (0, 0)),
                  pl.BlockSpec((1, t, ATTN_DV), lambda i, h, j: (i, j, h)),
                  pl.BlockSpec((1, lp, ATTN_DV), lambda i, h, j: (i, 0, h)),
                  pl.BlockSpec((1, lp, 2 * ATTN_DV), lambda i, h, j: (i, 0, h))],
        out_specs=pl.BlockSpec((1, t, ATTN_DV), lambda i, h, j: (i, j, h)),
        out_shape=jax.ShapeDtypeStruct((b, lp, ATTN_WIDTH), F32),
        scratch_shapes=[pltpu.VMEM((2 * t, ATTN_DV), BF16),
                        pltpu.VMEM((2 * t, LANES), F32),
                        pltpu.VMEM((2 * t, 2 * ATTN_DV), F32)],
        compiler_params=_cparams(("parallel", "parallel", "parallel")),
        name="attn_prompt",
    )(lamp, qb, kb, vb)


def _attn_sample_kernel(n_pages, s_len, pt_ref, lamp_ref, q_ref, kn_ref, vn_ref, *refs):
    kt_refs = refs[:n_pages]
    v_refs = refs[n_pages:2 * n_pages]
    o_ref, q2_sc, m_sc, l_sc, acc_sc = refs[2 * n_pages:]
    j = pl.program_id(1)
    hr = 2 * s_len

    @pl.when(j == 0)
    def _():
        q = q_ref[0]
        lane = lax.broadcasted_iota(jnp.int32, q.shape, 1)
        zero = jnp.zeros_like(q)
        q2_sc[...] = jnp.concatenate(
            [jnp.where((lane >= lo) & (lane < lo + ATTN_DK), q, zero) for lo in range(0, ATTN_WIDTH, ATTN_DK)], axis=0)
        m_sc[...] = jnp.full_like(m_sc, NEG_BIG)
        l_sc[...] = jnp.zeros_like(l_sc)
        acc_sc[...] = jnp.zeros_like(acc_sc)

    def update(s, values):
        tk = s.shape[1]
        m_prev = m_sc[...]
        m_new = jnp.maximum(m_prev, jnp.max(s, axis=1, keepdims=True))
        p = jnp.exp2(s - _lane_tile(m_new, tk // LANES))
        alpha = jnp.exp2(m_prev - m_new)
        l_sc[...] = alpha * l_sc[...] + jnp.sum(p, axis=1, keepdims=True)
        pb = p.astype(BF16)
        for h in range(N_HEADS):
            rs = slice(h * hr, (h + 1) * hr)
            acc_sc[rs, :] = alpha[rs] * acc_sc[rs, :] + jnp.dot(pb[rs], values(h), preferred_element_type=F32)
        m_sc[...] = m_new

    kt = jnp.concatenate([r[0] for r in kt_refs], axis=1).astype(BF16)
    s = jnp.dot(q2_sc[...], kt, preferred_element_type=F32)
    update(s, lambda h: jnp.concatenate([r[0, pl.ds(h, LANES, stride=N_HEADS), :] for r in v_refs],
                                        axis=0).astype(BF16))

    @pl.when(j == pl.num_programs(1) - 1)
    def _():
        lam = _lambda_value(lamp_ref)
        pad_k = jnp.zeros((LANES - s_len, ATTN_WIDTH), BF16)
        pad_v = jnp.zeros((LANES - s_len, ATTN_DV), BF16)
        s_own = lax.dot_general(q2_sc[...], jnp.concatenate([kn_ref[0], pad_k], axis=0),
                                (((1,), (1,)), ((), ())), preferred_element_type=F32)
        row = lax.broadcasted_iota(jnp.int32, s_own.shape, 0)
        col = lax.broadcasted_iota(jnp.int32, s_own.shape, 1)
        s_own = jnp.where(col <= (row & (s_len - 1)), s_own, NEG_BIG)
        update(s_own, lambda h: jnp.concatenate([vn_ref[0, :, 2 * h * ATTN_DV:(2 * h + 1) * ATTN_DV], pad_v], axis=0))
        o = acc_sc[...] / l_sc[...]
        for h in range(N_HEADS):
            o_ref[0, :, h * ATTN_DV:(h + 1) * ATTN_DV] = (o[h * hr:h * hr + s_len]
                                                          - lam * o[h * hr + s_len:(h + 1) * hr])


def _attn_sample(lamp, qb, kb, vb, cache_kt, cache_v, page_table, n_pages):
    db, s_len, _ = qb.shape
    n_past_pages = page_table.shape[1]
    assert cache_kt.shape[2] == LANES and s_len & (s_len - 1) == 0 and n_past_pages % n_pages == 0
    steps = n_past_pages // n_pages
    tok = lambda w: pl.BlockSpec((1, s_len, w), lambda i, j, pt: (i, 0, 0))

    def page_spec(p):
        return pl.BlockSpec((1, ATTN_WIDTH, LANES), lambda i, j, pt: (pt[i, j * n_pages + p], 0, 0))

    rows = 2 * s_len * N_HEADS
    return pl.pallas_call(
        functools.partial(_attn_sample_kernel, n_pages, s_len),
        grid_spec=pltpu.PrefetchScalarGridSpec(
            num_scalar_prefetch=1,
            grid=(db, steps),
            in_specs=[pl.BlockSpec(lamp.shape, lambda i, j, pt: (0, 0)),
                      tok(ATTN_WIDTH), tok(ATTN_WIDTH), tok(2 * ATTN_WIDTH)]
                     + [page_spec(p) for p in range(n_pages)] * 2,
            out_specs=tok(ATTN_WIDTH),
            scratch_shapes=[pltpu.VMEM((rows, ATTN_WIDTH), BF16),
                            pltpu.VMEM((rows, LANES), F32),
                            pltpu.VMEM((rows, LANES), F32),
                            pltpu.VMEM((rows, ATTN_DV), F32)]),
        out_shape=jax.ShapeDtypeStruct((db, s_len, ATTN_WIDTH), F32),
        compiler_params=_cparams(("parallel", "arbitrary")),
        name="attn_sample",
    )(page_table, lamp, qb, kb, vb, *([cache_kt] * n_pages), *([cache_v] * n_pages))


def _finish_kernel(h_ref, ys_ref, o_ref, sg_ref, wout_ref, g2_ref, wq_ref, keys_ref,
                   h1_ref, n2_ref, st_ref):
    o = o_ref[0]
    parts = []
    for h in range(N_HEADS):
        oh = o[:, h * ATTN_DV:(h + 1) * ATTN_DV]
        parts.append(oh * lax.rsqrt(jnp.mean(oh * oh, axis=-1, keepdims=True) + EPS))
    on = jnp.concatenate(parts, axis=1) * sg_ref[...] * (1.0 - LAMBDA_INIT)
    mix = jnp.concatenate([ys_ref[0], on.astype(BF16)], axis=1)
    h1 = h_ref[0] + jnp.dot(mix, wout_ref[...], preferred_element_type=F32)
    h1_ref[0] = h1
    n2 = (h1 * lax.rsqrt(jnp.mean(h1 * h1, axis=-1, keepdims=True) + EPS) * g2_ref[...]).astype(BF16)
    n2_ref[0] = n2
    qt = lax.dot_general(wq_ref[...], n2, (((1,), (1,)), ((), ())), preferred_element_type=F32)
    for hp in range(2 * PEER_HEADS):
        st_ref[0, hp] = jnp.dot(keys_ref[hp], qt[hp * N_KEYS:(hp + 1) * N_KEYS].astype(BF16),
                                preferred_element_type=F32)


def _finish(h, ys, o, tm, sg, wout, g2, wq_t, keys):
    b, lp, _ = h.shape
    nblk = lp // tm
    tok = lambda w: pl.BlockSpec((1, tm, w), lambda i, j: (i, j, 0))
    full = lambda a: pl.BlockSpec(a.shape, lambda i, j: (0,) * a.ndim)
    return pl.pallas_call(
        _finish_kernel,
        grid=(b, nblk),
        in_specs=[tok(D_MODEL), tok(SSM_WIDTH), tok(ATTN_WIDTH), full(sg), full(wout), full(g2),
                  full(wq_t), full(keys)],
        out_specs=[tok(D_MODEL), tok(D_MODEL),
                   pl.BlockSpec((1, 2 * PEER_HEADS, N_KEYS, tm), lambda i, j: (i, 0, 0, j))],
        out_shape=[jax.ShapeDtypeStruct((b, lp, D_MODEL), F32),
                   jax.ShapeDtypeStruct((b, lp, D_MODEL), BF16),
                   jax.ShapeDtypeStruct((b, 2 * PEER_HEADS, N_KEYS, lp), F32)],
        compiler_params=_cparams(("parallel", "parallel")),
        name="finish",
    )(h, ys, o, sg, wout, g2, wq_t, keys)


def _cmpx(v, i, j):
    hi = jnp.maximum(v[i], v[j])
    lo = jnp.minimum(v[i], v[j])
    v[i], v[j] = hi, lo


def _bitonic_merge_desc(v):
    n = len(v)
    v = list(v)
    d = n // 2
    while d >= 1:
        for i in range(n):
            if (i % (2 * d)) < d:
                _cmpx(v, i, i + d)
        d //= 2
    return v


def _sort_desc(v):
    n = len(v)
    if n == 1:
        return list(v)
    a = _sort_desc(v[:n // 2])
    b = _sort_desc(v[n // 2:])
    return _bitonic_merge_desc(a + b[::-1])


def _merge_top(t, s):
    n = len(t)
    m = list(t)
    for r in range(n):
        q = n - 1 - r
        if q < len(s):
            m[r] = jnp.maximum(t[r], s[q])
    return _bitonic_merge_desc(m)


def _pack_pair(lo, hi):
    lo_bits = lax.bitcast_convert_type(lo.astype(BF16).astype(F32), jnp.uint32)
    hi_bits = lax.bitcast_convert_type(hi.astype(BF16).astype(F32), jnp.uint32)
    return lax.bitcast_convert_type(hi_bits | (lo_bits >> 16), F32)


def _twice_bf16(x):
    return _pack_pair(x, x)


def _peer_select_kernel(s_ref, cnt_ref, w1_ref, rank_ref, e2_ref):
    tb = s_ref.shape[-1]
    sub = lax.broadcasted_iota(jnp.int32, (SUBLANES, LANES), 0)
    k = PEER_TOPK

    def lane_group(ln):
        tops = []
        for hp in range(2 * PEER_HEADS):
            col = _sort_desc([s_ref[0, hp, r * SUBLANES:(r + 1) * SUBLANES, ln] for r in range(N_KEYS // SUBLANES)])
            for shift in (4, 2, 1):
                other = [pltpu.roll(c, shift, 0) for c in col]
                col = _merge_top(col, other)
            tops.append(col)

        def pack(lists):
            out = []
            for r in range(k):
                x = lists[0][r]
                for h in range(1, PEER_HEADS):
                    x = jnp.where(sub == h, lists[h][r], x)
                out.append(x)
            return out

        a = pack([tops[2 * h] for h in range(PEER_HEADS)])
        b = pack([tops[2 * h + 1] for h in range(PEER_HEADS)])
        t = [a[0] + b[q] for q in range(k)]
        for i in range(1, k // 2):
            t = _merge_top(t, [a[i] + b[q] for q in range(k // (i + 1))])
        t = _merge_top(t, [a[i] + b[0] for i in range(k // 2, k)])
        theta = t[k - 1]
        z = jnp.ones_like(theta)
        for r in range(1, k):
            z = z + jnp.exp(t[r] - t[0])
        inv_z = 1.0 / z
        counts = []
        for r in range(k):
            cnt = jnp.zeros_like(theta)
            for q in range(k // (r + 1)):
                cnt = cnt + jnp.where(a[r] + b[q] >= theta, 1.0, 0.0)
            counts.append(cnt)

        for h in range(PEER_HEADS):
            bc = lambda x: jnp.broadcast_to(x[h:h + 1, :], (SUBLANES, LANES))
            a_h = tops[2 * h]
            b_h = tops[2 * h + 1]
            cnt_h = [bc(x) for x in counts]
            inv_z_h = bc(inv_z)
            for r in range(N_KEYS // SUBLANES):
                rs = slice(r * SUBLANES, (r + 1) * SUBLANES)
                s1 = s_ref[0, 2 * h, rs, ln]
                cnt = jnp.zeros_like(s1)
                for q in range(k):
                    cnt = jnp.where(s1 == a_h[q], cnt_h[q], cnt)
                cnt_ref[0, h, rs, ln] = _twice_bf16(cnt)
                w1_ref[0, h, rs, ln] = _twice_bf16(jnp.exp(s1 - a_h[0]) * inv_z_h)

            def second_key(start):
                s2 = s_ref[0, 2 * h + 1, pl.ds(start, SUBLANES, stride=2), ln]
                rank = jnp.full_like(s2, float(N_KEYS - 1))
                for q in range(k):
                    rank = jnp.where(s2 == b_h[q], float(q), rank)
                return rank, jnp.exp(s2 - b_h[0])

            for g in range(N_KEYS // BF16_ROWS):
                rank_even, e2_even = second_key(g * BF16_ROWS)
                rank_odd, e2_odd = second_key(g * BF16_ROWS + 1)
                ws = slice(g * SUBLANES, (g + 1) * SUBLANES)
                rank_ref[0, h, ws, ln] = _pack_pair(rank_even, rank_odd)
                e2_ref[0, h, ws, ln] = _pack_pair(e2_even, e2_odd)

    for g in range(tb // LANES):
        lane_group(slice(g * LANES, (g + 1) * LANES))


def _peer_select(st):
    tb = LANES
    b, _, _, lp = st.shape
    out = lambda rows: jax.ShapeDtypeStruct((b, PEER_HEADS, rows, lp), F32)
    spec = lambda rows: pl.BlockSpec((1, PEER_HEADS, rows, tb), lambda i, j: (i, 0, 0, j))
    return pl.pallas_call(
        _peer_select_kernel,
        grid=(b, lp // tb),
        in_specs=[pl.BlockSpec((1, 2 * PEER_HEADS, N_KEYS, tb), lambda i, j: (i, 0, 0, j))],
        out_specs=[spec(N_KEYS), spec(N_KEYS), spec(N_KEYS // 2), spec(N_KEYS // 2)],
        out_shape=[out(N_KEYS), out(N_KEYS), out(N_KEYS // 2), out(N_KEYS // 2)],
        compiler_params=_cparams(("parallel", "parallel")),
        name="peer_select",
    )(st)


def _peer_dense_kernel(ec, n2_ref, u_ref, vt_ref, rank_ref, e2_ref, cnt_ref, w1_ref, h1_ref,
                       out_ref, a_sc, w_sc, acc_sc):
    c = pl.program_id(2)
    tb = n2_ref.shape[1]

    @pl.when(c == 0)
    def _():
        acc_sc[...] = jnp.zeros_like(acc_sc)

    n2 = n2_ref[0]
    n_sub = ec // PEER_SUB

    def activations(sb):
        rows = slice(sb * PEER_SUB, (sb + 1) * PEER_SUB)
        a_sc[rows, :] = lax.dot_general(u_ref[rows, :], n2, (((1,), (1,)), ((), ())), preferred_element_type=F32)

    def gate(sb):
        il0 = sb * (PEER_SUB // N_KEYS)
        for cc in range(tb // LANES):
            ln = slice(cc * LANES, (cc + 1) * LANES)
            bc = lambda ref, h, il: pltpu.bitcast(jnp.broadcast_to(ref[0, h, il:il + 1, ln], (SUBLANES, LANES)), BF16)
            cnt = [[bc(cnt_ref, h, il0 + d) for h in range(PEER_HEADS)] for d in range(2)]
            w1 = [[bc(w1_ref, h, il0 + d) for h in range(PEER_HEADS)] for d in range(2)]
            for jp in range(N_KEYS // BF16_ROWS):
                ws = slice(jp * SUBLANES, (jp + 1) * SUBLANES)
                zero = jnp.zeros((BF16_ROWS, LANES), BF16)
                g = [zero, zero]
                for h in range(PEER_HEADS):
                    rank = pltpu.bitcast(rank_ref[0, h, ws, ln], BF16)
                    e2 = pltpu.bitcast(e2_ref[0, h, ws, ln], BF16)
                    for d in range(2):
                        g[d] = g[d] + jnp.where(rank < cnt[d][h], e2 * w1[d][h], zero)
                for d in range(2):
                    r0 = (il0 + d) * N_KEYS + jp * BF16_ROWS
                    w = g[d] * _gelu(a_sc[r0:r0 + BF16_ROWS, ln]).astype(BF16)
                    w_sc[r0 // 2:r0 // 2 + SUBLANES, ln] = pltpu.bitcast(w, F32)

    part = None
    activations(0)
    for sb in range(n_sub):
        gate(sb)
        if sb + 1 < n_sub:
            activations(sb + 1)
        rows = slice(sb * PEER_SUB, (sb + 1) * PEER_SUB)
        words = w_sc[sb * PEER_SUB // 2:(sb + 1) * PEER_SUB // 2, :]
        d = jnp.dot(vt_ref[:, rows], pltpu.bitcast(words, BF16), preferred_element_type=F32)
        part = d if part is None else part + d
    acc_sc[...] += part

    @pl.when(c == pl.num_programs(2) - 1)
    def _():
        out_ref[0] = h1_ref[0] + acc_sc[...].T


def _peer_dense(n2, u_bf, vt_bf, rank, e2, cnt, w1, h1, tb, ec):
    b, lp, _ = n2.shape
    rows = ec // N_KEYS
    keyed = pl.BlockSpec((1, PEER_HEADS, N_KEYS // 2, tb), lambda i, j, c: (i, 0, 0, j))
    chunk_rows = pl.BlockSpec((1, PEER_HEADS, rows, tb), lambda i, j, c: (i, 0, c, j))
    return pl.pallas_call(
        functools.partial(_peer_dense_kernel, ec),
        grid=(b, lp // tb, N_EXPERTS // ec),
        in_specs=[pl.BlockSpec((1, tb, D_MODEL), lambda i, j, c: (i, j, 0)),
                  pl.BlockSpec((ec, D_MODEL), lambda i, j, c: (c, 0)),
                  pl.BlockSpec((D_MODEL, ec), lambda i, j, c: (0, c)),
                  keyed, keyed, chunk_rows, chunk_rows,
                  pl.BlockSpec((1, tb, D_MODEL), lambda i, j, c: (i, j, 0))],
        out_specs=pl.BlockSpec((1, tb, D_MODEL), lambda i, j, c: (i, j, 0)),
        out_shape=jax.ShapeDtypeStruct((b, lp, D_MODEL), F32),
        scratch_shapes=[pltpu.VMEM((ec, tb), F32), pltpu.VMEM((ec // 2, tb), F32), pltpu.VMEM((D_MODEL, tb), F32)],
        compiler_params=_cparams(("parallel", "parallel", "arbitrary")),
        name="peer_dense",
    )(n2, u_bf, vt_bf, rank, e2, cnt, w1, h1)


def _rope_tables(pos):
    half = ATTN_DK // 2
    inv_freq = ROPE_THETA ** (-jnp.arange(half, dtype=F32) * 2.0 / ATTN_DK)
    ang = pos[:, None] * inv_freq[None, :]
    cos = jnp.cos(ang)
    sin = jnp.sin(ang)
    return (jnp.concatenate([cos, cos, cos, cos], axis=1),
            jnp.concatenate([-sin, sin, -sin, sin], axis=1))


def _round_up(x, m):
    return (x + m - 1) // m * m


def _layer(h, l_real, pos, weights, tm, tb, ec, attend):
    (g1, w_in, qg, kg, gm, ssm_c, sg, wout, g2, wq_t, keys, u_bf, vt_bf) = weights
    cos_t, sin_t = _rope_tables(pos)
    u, k, v, qb, kb, vb = _project(h, l_real, tm, g1, w_in, qg, kg, cos_t, sin_t, gm)
    o, ys, st = attend(u, qb, kb, vb, ssm_c)
    h1, n2, sc = _finish(h, ys, o, tb, sg, wout, g2, wq_t, keys)
    cnt, w1, rank, e2 = _peer_select(sc)
    out = _peer_dense(n2, u_bf, vt_bf, rank, e2, cnt, w1, h1, tb, ec)
    return out, k, v, st


def kernel(x_prompt, x_sample, cache_k, cache_v, state_ssm_re, state_ssm_im, page_table, meta_tokens, norm1_g, w_in, q_norm_g, k_norm_g, lambda_q1, lambda_k1, lambda_q2, lambda_k2, subln_g, ssm_A_re, ssm_A_im, ssm_log_dt, ssm_B_re, ssm_B_im, ssm_C_re, ssm_C_im, ssm_D, w_glu, b_glu, w_out, norm2_g, peer_w_q, peer_keys, peer_u, peer_v):
    assert w_in.shape[0] == 1, "single-layer trunk"
    bp, seq, _ = x_prompt.shape
    db, ds, _ = x_sample.shape
    lp_real = seq + N_META
    assert lp_real % SUBLANES == 0 and ds == SUBLANES
    tm_p, t_attn, tb, ec = 768, 512, 512, 2048
    lp = _round_up(lp_real, math.lcm(tm_p, t_attn, tb))
    past = page_table.shape[1] * cache_k.shape[2]

    gm = jnp.kron(jnp.eye(ATTN_WIDTH // ATTN_DK, dtype=F32),
                  jnp.full((ATTN_DK, ATTN_DK), 1.0 / ATTN_DK, F32)).astype(BF16)
    lamp = jnp.stack([lambda_q1[0], lambda_k1[0], lambda_q2[0], lambda_k2[0]]).astype(F32)
    ssm_c = _ssm_consts(ssm_A_re[0], ssm_A_im[0], ssm_log_dt[0], ssm_B_re[0], ssm_B_im[0],
                        ssm_C_re[0], ssm_C_im[0], ssm_D[0], w_glu[0], b_glu[0])
    weights = (norm1_g[0].astype(F32)[None, :], w_in[0].astype(BF16),
               jnp.tile(q_norm_g[0].astype(F32), ATTN_WIDTH // ATTN_DK)[None, :],
               jnp.tile(k_norm_g[0].astype(F32), ATTN_WIDTH // ATTN_DK)[None, :],
               gm, ssm_c,
               jnp.tile(subln_g[0].astype(F32), N_HEADS)[None, :], w_out[0].astype(BF16),
               norm2_g[0].astype(F32)[None, :], peer_w_q[0].T.astype(BF16),
               peer_keys[0].reshape(2 * PEER_HEADS, N_KEYS, N_KEYS).astype(BF16),
               peer_u[0].astype(BF16), peer_v[0].T.astype(BF16))

    hp = jnp.concatenate([jnp.broadcast_to(meta_tokens.astype(F32)[None], (bp, N_META, D_MODEL)),
                          x_prompt.astype(F32),
                          jnp.zeros((bp, lp - lp_real, D_MODEL), F32)], axis=1)

    def attend_prompt(u, qb, kb, vb, consts):
        ys, st = _ssm(u, tm_p, lp_real - 1, consts)
        return _attn_prompt(lamp, qb, kb, vb, t_attn), ys, st

    hp_out, k_p, v_p, st_p = _layer(hp, lp_real, jnp.arange(lp, dtype=F32), weights, tm_p, tb, ec, attend_prompt)

    n_tok = db * ds
    hs = x_sample.astype(F32).reshape(1, n_tok, D_MODEL)
    h0 = jnp.concatenate([state_ssm_re[0].reshape(db, SSM_FLAT), state_ssm_im[0].reshape(db, SSM_FLAT)],
                         axis=1).astype(F32)[None]
    n_pool, page = cache_k.shape[1], cache_k.shape[2]
    ckt = jnp.transpose(cache_k[0], (0, 2, 3, 4, 1)).reshape(n_pool, ATTN_WIDTH, page)
    cv = cache_v[0].reshape(n_pool, page * N_HEADS, ATTN_DV)

    def attend_sample(u, qb, kb, vb, consts):
        ys, st = _ssm(u, 256, None, consts, h0=h0)
        seqs = lambda a: a.reshape(db, ds, a.shape[-1])
        o = _attn_sample(lamp, seqs(qb), seqs(kb), seqs(vb), ckt, cv, page_table, 16)
        return o.reshape(1, n_tok, ATTN_WIDTH), ys, st

    pos_s = jnp.tile(past + jnp.arange(ds, dtype=F32), db)
    hs_out, k_s, v_s, st_s = _layer(hs, n_tok, pos_s, weights, 512, tb, ec, attend_sample)

    y_prompt = hp_out[:, N_META:lp_real].astype(x_prompt.dtype)
    y_sample = hs_out.reshape(db, ds, D_MODEL).astype(x_sample.dtype)
    row = (lp_real - 1) % SUBLANES
    kd, vd, sd = cache_k.dtype, cache_v.dtype, state_ssm_re.dtype
    return (y_prompt, y_sample,
            k_p.reshape(1, bp, lp_real, N_HEADS, 2, ATTN_DK).astype(kd),
            v_p.reshape(1, bp, lp_real, N_HEADS, ATTN_DV).astype(vd),
            st_p[:, row, :SSM_FLAT].reshape(1, bp, SSM_GROUPS, SSM_STATE).astype(sd),
            st_p[:, row, SSM_FLAT:].reshape(1, bp, SSM_GROUPS, SSM_STATE).astype(state_ssm_im.dtype),
            k_s.reshape(1, db, ds, N_HEADS, 2, ATTN_DK).astype(kd),
            v_s.reshape(1, db, ds, N_HEADS, ATTN_DV).astype(vd),
            st_s[0, ds - 1::ds, :SSM_FLAT].reshape(1, db, SSM_GROUPS, SSM_STATE).astype(sd),
            st_s[0, ds - 1::ds, SSM_FLAT:].reshape(1, db, SSM_GROUPS, SSM_STATE).astype(state_ssm_im.dtype))
```

```python
import functools
import math

import jax
import jax.numpy as jnp
from jax import lax
from jax.experimental import pallas as pl
from jax.experimental.pallas import tpu as pltpu

F32 = jnp.float32
BF16 = jnp.bfloat16

D_MODEL = 1024
N_META = 16
SSM_WIDTH = 512
SSM_GROUP = 16
SSM_GROUPS = 32
SSM_STATE = 64
SSM_FLAT = SSM_GROUPS * SSM_STATE
ATTN_WIDTH = 512
ATTN_DV = 128
N_HEADS = 4
ATTN_DK = 64
IN_WIDTH = 2048
ROPE_THETA = 10000.0
N_KEYS = 128
N_EXPERTS = N_KEYS * N_KEYS
PEER_HEADS = 8
PEER_TOPK = 16
PEER_SUB = 2 * N_KEYS
EPS = 1e-6
NEG_BIG = -1e30
POS_BIG = 3e38
LAMBDA_INIT = 0.8 - 0.6 * math.exp(-0.3 * 0)
Q_SCALE = ATTN_DK ** -0.5 * math.log2(math.e)
ATTN_GROUP = 4

SUBLANES = 8
LANES = 128
BF16_ROWS = 2 * SUBLANES
VMEM_LIMIT = 56 * 1024 * 1024


def _cparams(sem):
    return pltpu.CompilerParams(dimension_semantics=sem, vmem_limit_bytes=VMEM_LIMIT)


GELU_C1 = 0.7978845608028654
GELU_C2 = GELU_C1 * 0.044715


def _gelu(x):
    return x * (0.5 + 0.5 * jnp.tanh(x * (GELU_C1 + GELU_C2 * (x * x))))


def _proj_kernel(x_ref, g1_ref, w_ref, qg_ref, kg_ref, cos_ref, sin_ref, gm_ref,
                 u_ref, k_ref, v_ref, qb_ref, kb_ref, vb_ref):
    x = x_ref[0]
    tm = x.shape[0]
    ms = jnp.mean(x * x, axis=-1, keepdims=True)
    n = (x * lax.rsqrt(ms + EPS) * g1_ref[...]).astype(BF16)
    z = jnp.dot(n, w_ref[...], preferred_element_type=F32)
    u_ref[0] = z[:, :SSM_WIDTH]
    v = z[:, SSM_WIDTH + 2 * ATTN_WIDTH:]
    v_ref[0] = v
    vb = v.astype(BF16)
    ones = jnp.ones((tm, ATTN_DV), BF16)
    vb_ref[0] = jnp.concatenate([t for h in range(N_HEADS) for t in (vb[:, h * ATTN_DV:(h + 1) * ATTN_DV], ones)], axis=1)

    cos = jnp.concatenate([cos_ref[...]] * 4, axis=1)
    sin = jnp.concatenate([sin_ref[...]] * 4, axis=1)
    lane = lax.broadcasted_iota(jnp.int32, (tm, ATTN_WIDTH), 1)
    first_half = (lane & (ATTN_DK - 1)) < (ATTN_DK // 2)
    gm = gm_ref[...]

    def norm_rope(t, g):
        sq = t * t
        hi = sq.astype(BF16)
        lo = (sq - hi.astype(F32)).astype(BF16)
        ms64 = (jnp.dot(hi, gm, preferred_element_type=F32)
                + jnp.dot(lo, gm, preferred_element_type=F32))
        tn = t * lax.rsqrt(ms64 + EPS) * g
        swapped = jnp.where(first_half,
                            pltpu.roll(tn, ATTN_WIDTH - ATTN_DK // 2, 1),
                            pltpu.roll(tn, ATTN_DK // 2, 1))
        return tn * cos + swapped * sin

    q = norm_rope(z[:, SSM_WIDTH:SSM_WIDTH + ATTN_WIDTH], qg_ref[...])
    k = norm_rope(z[:, SSM_WIDTH + ATTN_WIDTH:SSM_WIDTH + 2 * ATTN_WIDTH], kg_ref[...])
    k_ref[0] = k
    kb_ref[0] = k.astype(BF16)
    qb_ref[0] = (q * Q_SCALE).astype(BF16)


def _project(h, l_out, tm, g1, w_bf, qg, kg, cos_t, sin_t, gm):
    b, lp, _ = h.shape
    nblk = lp // tm
    tok = lambda w: pl.BlockSpec((1, tm, w), lambda i, j: (i, j, 0))
    full = lambda a: pl.BlockSpec(a.shape, lambda i, j: (0,) * a.ndim)
    return pl.pallas_call(
        _proj_kernel,
        grid=(b, nblk),
        in_specs=[tok(D_MODEL), full(g1), full(w_bf), full(qg), full(kg),
                  pl.BlockSpec((tm, LANES), lambda i, j: (j, 0)),
                  pl.BlockSpec((tm, LANES), lambda i, j: (j, 0)), full(gm)],
        out_specs=[tok(SSM_WIDTH), tok(ATTN_WIDTH), tok(ATTN_WIDTH),
                   tok(ATTN_WIDTH), tok(ATTN_WIDTH), tok(2 * ATTN_WIDTH)],
        out_shape=[jax.ShapeDtypeStruct((b, lp, SSM_WIDTH), F32),
                   jax.ShapeDtypeStruct((b, l_out, ATTN_WIDTH), F32),
                   jax.ShapeDtypeStruct((b, l_out, ATTN_WIDTH), F32),
                   jax.ShapeDtypeStruct((b, lp, ATTN_WIDTH), BF16),
                   jax.ShapeDtypeStruct((b, lp, ATTN_WIDTH), BF16),
                   jax.ShapeDtypeStruct((b, lp, 2 * ATTN_WIDTH), BF16)],
        compiler_params=_cparams(("parallel", "parallel")),
        name="project",
    )(h, g1, w_bf, qg, kg, cos_t, sin_t, gm)


SSM_CHUNK = 512


def _ssm_kernel(sequential, state_row, state_blk, *refs):
    if sequential:
        (u_ref, bbd_ref, cbd_ref, kc_ref, pc_ref, d_ref, wg_ref, bg_ref,
         y_ref, st_ref, x_sc, carry_sc) = refs
        h0_ref = None
    else:
        (u_ref, bbd_ref, cbd_ref, kc_ref, pc_ref, d_ref, wg_ref, bg_ref, h0_ref,
         y_ref, st_ref, x_sc) = refs
        carry_sc = None
    j = pl.program_id(1)
    u = u_ref[0]
    tm = u.shape[0]
    ub = u.astype(BF16)
    hw, hf = SSM_WIDTH // 2, SSM_FLAT // 2
    for half in range(2):
        xh = jnp.dot(ub[:, half * hw:(half + 1) * hw], bbd_ref[half], preferred_element_type=F32)
        x_sc[:, half * hf:(half + 1) * hf] = xh[:, :hf]
        x_sc[:, SSM_FLAT + half * hf:SSM_FLAT + (half + 1) * hf] = xh[:, hf:]

    if sequential:
        @pl.when(j == 0)
        def _():
            carry_sc[...] = jnp.zeros_like(carry_sc)

    def tile_body(i, carry):
        r0 = pl.multiple_of(i * SUBLANES, SUBLANES)
        for c in range(0, SSM_FLAT, SSM_CHUNK):
            re = slice(c, c + SSM_CHUNK)
            im = slice(SSM_FLAT + c, SSM_FLAT + c + SSM_CHUNK)
            xr = x_sc[pl.ds(r0, SUBLANES), re]
            xi = x_sc[pl.ds(r0, SUBLANES), im]
            for di, d in enumerate((1, 2, 4)):
                cr = kc_ref[di, 0, :, re]
                ci = kc_ref[di, 1, :, re]
                rr = pltpu.roll(xr, d, 0)
                ri = pltpu.roll(xi, d, 0)
                xr, xi = xr + cr * rr - ci * ri, xi + cr * ri + ci * rr
            if sequential:
                car_r = carry_sc[:, re]
                car_i = carry_sc[:, im]
            else:
                car_r = h0_ref[0, pl.ds(r0, SUBLANES), re]
                car_i = h0_ref[0, pl.ds(r0, SUBLANES), im]
            pr = pc_ref[0, :, re]
            pi = pc_ref[1, :, re]
            xr, xi = xr + pr * car_r - pi * car_i, xi + pr * car_i + pi * car_r
            x_sc[pl.ds(r0, SUBLANES), re] = xr
            x_sc[pl.ds(r0, SUBLANES), im] = xi
            if sequential:
                carry_sc[:, re] = jnp.broadcast_to(xr[SUBLANES - 1:, :], (SUBLANES, SSM_CHUNK))
                carry_sc[:, im] = jnp.broadcast_to(xi[SUBLANES - 1:, :], (SUBLANES, SSM_CHUNK))
        return carry

    lax.fori_loop(0, tm // SUBLANES, tile_body, 0)

    if sequential:
        @pl.when(j == state_blk)
        def _():
            st_ref[0] = x_sc[state_row:state_row + SUBLANES, :]
    else:
        st_ref[0] = x_sc[...]

    ys = []
    for half in range(2):
        x_re = x_sc[:, half * hf:(half + 1) * hf].astype(BF16)
        x_im = x_sc[:, SSM_FLAT + half * hf:SSM_FLAT + (half + 1) * hf].astype(BF16)
        ys.append(jnp.dot(x_re, cbd_ref[half, :hf], preferred_element_type=F32)
                  + jnp.dot(x_im, cbd_ref[half, hf:], preferred_element_type=F32))
    y = jnp.concatenate(ys, axis=1) + d_ref[...] * u
    y = _gelu(y)
    gate = jnp.dot(y.astype(BF16), wg_ref[...], preferred_element_type=F32) + bg_ref[...]
    y_ref[0] = (y * (1.0 / (1.0 + jnp.exp(-gate)))).astype(BF16)


def _ssm(u, tm, last_token, consts, h0=None):
    bbd, cbd, kc, pc, dsk, wg, bg = consts
    b, lp, _ = u.shape
    nblk = lp // tm
    sequential = h0 is None
    full = lambda a: pl.BlockSpec(a.shape, lambda i, j: (0,) * a.ndim)
    in_specs = [pl.BlockSpec((1, tm, SSM_WIDTH), lambda i, j: (i, j, 0)),
                full(bbd), full(cbd), full(kc), full(pc), full(dsk), full(wg), full(bg)]
    args = [u, bbd, cbd, kc, pc, dsk, wg, bg]
    scratch = [pltpu.VMEM((tm, 2 * SSM_FLAT), F32)]
    if sequential:
        state_blk = last_token // tm
        state_row = (last_token % tm) // SUBLANES * SUBLANES
        st_spec = pl.BlockSpec((1, SUBLANES, 2 * SSM_FLAT), lambda i, j: (i, 0, 0))
        st_shape = jax.ShapeDtypeStruct((b, SUBLANES, 2 * SSM_FLAT), F32)
        scratch.append(pltpu.VMEM((SUBLANES, 2 * SSM_FLAT), F32))
        sem = ("parallel", "arbitrary")
    else:
        state_blk = state_row = 0
        in_specs.append(pl.BlockSpec((1, tm, 2 * SSM_FLAT), lambda i, j: (i, j, 0)))
        args.append(jnp.repeat(h0, SUBLANES, axis=1))
        st_spec = pl.BlockSpec((1, tm, 2 * SSM_FLAT), lambda i, j: (i, j, 0))
        st_shape = jax.ShapeDtypeStruct((b, lp, 2 * SSM_FLAT), F32)
        sem = ("parallel", "parallel")
    return pl.pallas_call(
        functools.partial(_ssm_kernel, sequential, state_row, state_blk),
        grid=(b, nblk),
        in_specs=in_specs,
        out_specs=[pl.BlockSpec((1, tm, SSM_WIDTH), lambda i, j: (i, j, 0)), st_spec],
        out_shape=[jax.ShapeDtypeStruct((b, lp, SSM_WIDTH), BF16), st_shape],
        scratch_shapes=scratch,
        compiler_params=_cparams(sem),
        name="s5_prompt" if sequential else "s5_sample",
    )(*args)


def _ssm_consts(a_re, a_im, log_dt, b_re, b_im, c_re, c_im, d_skip, w_glu, b_glu):
    dt = jnp.exp(log_dt.astype(F32))[:, None]
    a_re = a_re.astype(F32)
    a_im = a_im.astype(F32)
    mag = jnp.exp(dt * a_re)
    ab_re = mag * jnp.cos(dt * a_im)
    ab_im = mag * jnp.sin(dt * a_im)
    den = a_re * a_re + a_im * a_im
    z_re = ((ab_re - 1.0) * a_re + ab_im * a_im) / den
    z_im = (ab_im * a_re - (ab_re - 1.0) * a_im) / den
    b_re = b_re.astype(F32)
    b_im = b_im.astype(F32)
    bb_re = z_re[..., None] * b_re - z_im[..., None] * b_im
    bb_im = z_re[..., None] * b_im + z_im[..., None] * b_re
    eye = jnp.eye(SSM_GROUPS, dtype=F32)
    bbd = jnp.concatenate(
        [jnp.einsum('gnc,gh->gchn', bb_re, eye).reshape(SSM_WIDTH, SSM_FLAT),
         jnp.einsum('gnc,gh->gchn', bb_im, eye).reshape(SSM_WIDTH, SSM_FLAT)], axis=1).astype(BF16)
    cbd = jnp.concatenate(
        [jnp.einsum('gcn,gh->gnhc', c_re.astype(F32), eye).reshape(SSM_FLAT, SSM_WIDTH),
         -jnp.einsum('gcn,gh->gnhc', c_im.astype(F32), eye).reshape(SSM_FLAT, SSM_WIDTH)], axis=0).astype(BF16)
    hw, hf = SSM_WIDTH // 2, SSM_FLAT // 2
    half_b = lambda k: jnp.concatenate([bbd[k * hw:(k + 1) * hw, k * hf:(k + 1) * hf],
                                        bbd[k * hw:(k + 1) * hw, SSM_FLAT + k * hf:SSM_FLAT + (k + 1) * hf]], axis=1)
    half_c = lambda k: jnp.concatenate([cbd[k * hf:(k + 1) * hf, k * hw:(k + 1) * hw],
                                        cbd[SSM_FLAT + k * hf:SSM_FLAT + (k + 1) * hf, k * hw:(k + 1) * hw]], axis=0)
    bbd = jnp.stack([half_b(0), half_b(1)])
    cbd = jnp.stack([half_c(0), half_c(1)])

    def power(p):
        m = jnp.exp(p * dt * a_re)
        return (m * jnp.cos(p * dt * a_im)).reshape(-1), (m * jnp.sin(p * dt * a_im)).reshape(-1)

    row = jnp.arange(SUBLANES)[:, None]
    kc = []
    for d in (1, 2, 4):
        pr, pi = power(float(d))
        kc.append(jnp.stack([jnp.where(row >= d, pr[None, :], 0.0), jnp.where(row >= d, pi[None, :], 0.0)]))
    kc = jnp.stack(kc)
    rows = [power(float(s + 1)) for s in range(SUBLANES)]
    pc = jnp.stack([jnp.stack([r[0] for r in rows]), jnp.stack([r[1] for r in rows])])
    return (bbd, cbd, kc, pc, d_skip.astype(F32)[None, :], w_glu.astype(BF16), b_glu.astype(F32)[None, :])


def _lambda_value(lamp_ref):
    lp = lamp_ref[...]
    s1 = jnp.sum(lp[0:1] * lp[1:2], axis=1, keepdims=True)
    s2 = jnp.sum(lp[2:3] * lp[3:4], axis=1, keepdims=True)
    return jnp.exp(s1) - jnp.exp(s2) + LAMBDA_INIT


def _stack_sub_queries(q):
    lane = lax.broadcasted_iota(jnp.int32, q.shape, 1)
    zero = jnp.zeros_like(q)
    return jnp.concatenate([jnp.where(lane < ATTN_DK, q, zero), jnp.where(lane >= ATTN_DK, q, zero)], axis=0)


def _lane_tile(x, n):
    return x if n == 1 else jnp.concatenate([x] * n, axis=1)


def _flash_update(s, v, m_sc, acc_sc):
    tk = s.shape[1]
    m_prev = m_sc[...]
    m_new = jnp.maximum(m_prev, jnp.max(s, axis=1, keepdims=True))
    p = jnp.exp2(s - _lane_tile(m_new, tk // LANES))
    alpha = jnp.exp2(m_prev - m_new)
    acc_sc[...] = _lane_tile(alpha, 2) * acc_sc[...] + jnp.dot(p.astype(BF16), v, preferred_element_type=F32)
    m_sc[...] = m_new


def _attn_prompt_kernel(t, lamp_ref, q_ref, k_ref, v_ref, o_ref, q2_sc, m_sc, acc_sc):
    iq = pl.program_id(2)
    q2_sc[...] = _stack_sub_queries(q_ref[0])
    m_sc[...] = jnp.full_like(m_sc, NEG_BIG)
    acc_sc[...] = jnp.zeros_like(acc_sc)

    def scores(ik):
        k0 = pl.multiple_of(ik * t, t)
        s = lax.dot_general(q2_sc[...], k_ref[0, pl.ds(k0, t), :], (((1,), (1,)), ((), ())),
                            preferred_element_type=F32)
        return s, v_ref[0, pl.ds(k0, t), :]

    def causal(s):
        row = lax.broadcasted_iota(jnp.int32, s.shape, 0)
        col = lax.broadcasted_iota(jnp.int32, s.shape, 1)
        return jnp.where(col <= jnp.where(row >= t, row - t, row), s, NEG_BIG)

    def tiles(indices, diag_last):
        sv = [scores(i) for i in indices]
        for n, (s, v) in enumerate(sv):
            _flash_update(causal(s) if diag_last and n == len(sv) - 1 else s, v, m_sc, acc_sc)

    def group_body(i, c):
        tiles([ATTN_GROUP * i + n for n in range(ATTN_GROUP)], False)
        return c

    lax.fori_loop(0, iq // ATTN_GROUP, group_body, 0)

    for rest in range(ATTN_GROUP):
        @pl.when(iq % ATTN_GROUP == rest)
        def _():
            tiles([iq - rest + n for n in range(rest)] + [iq], True)

    lam = _lambda_value(lamp_ref)
    acc = acc_sc[...]
    o = acc[:, :ATTN_DV] / acc[:, ATTN_DV:]
    o_ref[0] = o[:t] - lam * o[t:]


def _attn_prompt(lamp, qb, kb, vb, t):
    b, lp, _ = qb.shape
    return pl.pallas_call(
        functools.partial(_attn_prompt_kernel, t),
        grid=(b, N_HEADS, lp // t),
        in_specs=[pl.BlockSpec(lamp.shape, lambda i, h, j: (0, 0)),
                  pl.BlockSpec((1, t, ATTN_DV), lambda i, h, j: (i, j, h)),
                  pl.BlockSpec((1, lp, ATTN_DV), lambda i, h, j: (i, 0, h)),
                  pl.BlockSpec((1, lp, 2 * ATTN_DV), lambda i, h, j: (i, 0, h))],
        out_specs=pl.BlockSpec((1, t, ATTN_DV), lambda i, h, j: (i, j, h)),
        out_shape=jax.ShapeDtypeStruct((b, lp, ATTN_WIDTH), F32),
        scratch_shapes=[pltpu.VMEM((2 * t, ATTN_DV), BF16),
                        pltpu.VMEM((2 * t, LANES), F32),
                        pltpu.VMEM((2 * t, 2 * ATTN_DV), F32)],
        compiler_params=_cparams(("parallel", "parallel", "parallel")),
        name="attn_prompt",
    )(lamp, qb, kb, vb)


def _attn_sample_kernel(n_pages, s_len, pt_ref, lamp_ref, q_ref, kn_ref, vn_ref, *refs):
    kt_refs = refs[:n_pages]
    v_refs = refs[n_pages:2 * n_pages]
    o_ref, q2_sc, m_sc, l_sc, acc_sc = refs[2 * n_pages:]
    j = pl.program_id(1)
    hr = 2 * s_len

    @pl.when(j == 0)
    def _():
        q = q_ref[0]
        lane = lax.broadcasted_iota(jnp.int32, q.shape, 1)
        zero = jnp.zeros_like(q)
        q2_sc[...] = jnp.concatenate(
            [jnp.where((lane >= lo) & (lane < lo + ATTN_DK), q, zero) for lo in range(0, ATTN_WIDTH, ATTN_DK)], axis=0)
        m_sc[...] = jnp.full_like(m_sc, NEG_BIG)
        l_sc[...] = jnp.zeros_like(l_sc)
        acc_sc[...] = jnp.zeros_like(acc_sc)

    def update(s, values):
        tk = s.shape[1]
        m_prev = m_sc[...]
        m_new = jnp.maximum(m_prev, jnp.max(s, axis=1, keepdims=True))
        p = jnp.exp2(s - _lane_tile(m_new, tk // LANES))
        alpha = jnp.exp2(m_prev - m_new)
        l_sc[...] = alpha * l_sc[...] + jnp.sum(p, axis=1, keepdims=True)
        pb = p.astype(BF16)
        for h in range(N_HEADS):
            rs = slice(h * hr, (h + 1) * hr)
            acc_sc[rs, :] = alpha[rs] * acc_sc[rs, :] + jnp.dot(pb[rs], values(h), preferred_element_type=F32)
        m_sc[...] = m_new

    kt = jnp.concatenate([r[0] for r in kt_refs], axis=1).astype(BF16)
    s = jnp.dot(q2_sc[...], kt, preferred_element_type=F32)
    update(s, lambda h: jnp.concatenate([r[0, pl.ds(h, LANES, stride=N_HEADS), :] for r in v_refs],
                                        axis=0).astype(BF16))

    @pl.when(j == pl.num_programs(1) - 1)
    def _():
        lam = _lambda_value(lamp_ref)
        pad_k = jnp.zeros((LANES - s_len, ATTN_WIDTH), BF16)
        pad_v = jnp.zeros((LANES - s_len, ATTN_DV), BF16)
        s_own = lax.dot_general(q2_sc[...], jnp.concatenate([kn_ref[0], pad_k], axis=0),
                                (((1,), (1,)), ((), ())), preferred_element_type=F32)
        row = lax.broadcasted_iota(jnp.int32, s_own.shape, 0)
        col = lax.broadcasted_iota(jnp.int32, s_own.shape, 1)
        s_own = jnp.where(col <= (row & (s_len - 1)), s_own, NEG_BIG)
        update(s_own, lambda h: jnp.concatenate([vn_ref[0, :, 2 * h * ATTN_DV:(2 * h + 1) * ATTN_DV], pad_v], axis=0))
        o = acc_sc[...] / l_sc[...]
        for h in range(N_HEADS):
            o_ref[0, :, h * ATTN_DV:(h + 1) * ATTN_DV] = (o[h * hr:h * hr + s_len]
                                                          - lam * o[h * hr + s_len:(h + 1) * hr])


def _attn_sample(lamp, qb, kb, vb, cache_kt, cache_v, page_table, n_pages):
    db, s_len, _ = qb.shape
    n_past_pages = page_table.shape[1]
    assert cache_kt.shape[2] == LANES and s_len & (s_len - 1) == 0 and n_past_pages % n_pages == 0
    steps = n_past_pages // n_pages
    tok = lambda w: pl.BlockSpec((1, s_len, w), lambda i, j, pt: (i, 0, 0))

    def page_spec(p):
        return pl.BlockSpec((1, ATTN_WIDTH, LANES), lambda i, j, pt: (pt[i, j * n_pages + p], 0, 0))

    rows = 2 * s_len * N_HEADS
    return pl.pallas_call(
        functools.partial(_attn_sample_kernel, n_pages, s_len),
        grid_spec=pltpu.PrefetchScalarGridSpec(
            num_scalar_prefetch=1,
            grid=(db, steps),
            in_specs=[pl.BlockSpec(lamp.shape, lambda i, j, pt: (0, 0)),
                      tok(ATTN_WIDTH), tok(ATTN_WIDTH), tok(2 * ATTN_WIDTH)]
                     + [page_spec(p) for p in range(n_pages)] * 2,
            out_specs=tok(ATTN_WIDTH),
            scratch_shapes=[pltpu.VMEM((rows, ATTN_WIDTH), BF16),
                            pltpu.VMEM((rows, LANES), F32),
                            pltpu.VMEM((rows, LANES), F32),
                            pltpu.VMEM((rows, ATTN_DV), F32)]),
        out_shape=jax.ShapeDtypeStruct((db, s_len, ATTN_WIDTH), F32),
        compiler_params=_cparams(("parallel", "arbitrary")),
        name="attn_sample",
    )(page_table, lamp, qb, kb, vb, *([cache_kt] * n_pages), *([cache_v] * n_pages))


def _finish_kernel(h_ref, ys_ref, o_ref, sg_ref, wout_ref, g2_ref, wq_ref, keys_ref,
                   h1_ref, n2_ref, st_ref):
    o = o_ref[0]
    parts = []
    for h in range(N_HEADS):
        oh = o[:, h * ATTN_DV:(h + 1) * ATTN_DV]
        parts.append(oh * lax.rsqrt(jnp.mean(oh * oh, axis=-1, keepdims=True) + EPS))
    on = jnp.concatenate(parts, axis=1) * sg_ref[...] * (1.0 - LAMBDA_INIT)
    mix = jnp.concatenate([ys_ref[0], on.astype(BF16)], axis=1)
    h1 = h_ref[0] + jnp.dot(mix, wout_ref[...], preferred_element_type=F32)
    h1_ref[0] = h1
    n2 = h1 * lax.rsqrt(jnp.mean(h1 * h1, axis=-1, keepdims=True) + EPS) * g2_ref[...]
    n2t = n2.T.astype(BF16)
    n2_ref[0] = n2t
    qt = jnp.dot(wq_ref[...], n2t, preferred_element_type=F32)
    for hp in range(2 * PEER_HEADS):
        st_ref[0, hp] = jnp.dot(keys_ref[hp], qt[hp * N_KEYS:(hp + 1) * N_KEYS].astype(BF16),
                                preferred_element_type=F32)


def _finish(h, ys, o, tm, sg, wout, g2, wq_t, keys):
    b, lp, _ = h.shape
    nblk = lp // tm
    tok = lambda w: pl.BlockSpec((1, tm, w), lambda i, j: (i, j, 0))
    full = lambda a: pl.BlockSpec(a.shape, lambda i, j: (0,) * a.ndim)
    return pl.pallas_call(
        _finish_kernel,
        grid=(b, nblk),
        in_specs=[tok(D_MODEL), tok(SSM_WIDTH), tok(ATTN_WIDTH), full(sg), full(wout), full(g2),
                  full(wq_t), full(keys)],
        out_specs=[tok(D_MODEL), pl.BlockSpec((1, D_MODEL, tm), lambda i, j: (i, 0, j)),
                   pl.BlockSpec((1, 2 * PEER_HEADS, N_KEYS, tm), lambda i, j: (i, 0, 0, j))],
        out_shape=[jax.ShapeDtypeStruct((b, lp, D_MODEL), F32),
                   jax.ShapeDtypeStruct((b, D_MODEL, lp), BF16),
                   jax.ShapeDtypeStruct((b, 2 * PEER_HEADS, N_KEYS, lp), F32)],
        compiler_params=_cparams(("parallel", "parallel")),
        name="finish",
    )(h, ys, o, sg, wout, g2, wq_t, keys)


def _cmpx(v, i, j):
    hi = jnp.maximum(v[i], v[j])
    lo = jnp.minimum(v[i], v[j])
    v[i], v[j] = hi, lo


def _bitonic_merge_desc(v):
    n = len(v)
    v = list(v)
    d = n // 2
    while d >= 1:
        for i in range(n):
            if (i % (2 * d)) < d:
                _cmpx(v, i, i + d)
        d //= 2
    return v


def _sort_desc(v):
    n = len(v)
    if n == 1:
        return list(v)
    a = _sort_desc(v[:n // 2])
    b = _sort_desc(v[n // 2:])
    return _bitonic_merge_desc(a + b[::-1])


def _merge_top(t, s):
    n = len(t)
    m = list(t)
    for r in range(n):
        q = n - 1 - r
        if q < len(s):
            m[r] = jnp.maximum(t[r], s[q])
    return _bitonic_merge_desc(m)


def _pack_pair(lo, hi):
    lo_bits = lax.bitcast_convert_type(lo.astype(BF16).astype(F32), jnp.uint32)
    hi_bits = lax.bitcast_convert_type(hi.astype(BF16).astype(F32), jnp.uint32)
    return lax.bitcast_convert_type(hi_bits | (lo_bits >> 16), F32)


def _twice_bf16(x):
    return _pack_pair(x, x)


def _peer_select_kernel(s_ref, cnt_ref, w1_ref, rank_ref, e2_ref):
    tb = s_ref.shape[-1]
    sub = lax.broadcasted_iota(jnp.int32, (SUBLANES, LANES), 0)
    k = PEER_TOPK

    def lane_group(ln):
        tops = []
        for hp in range(2 * PEER_HEADS):
            col = _sort_desc([s_ref[0, hp, r * SUBLANES:(r + 1) * SUBLANES, ln] for r in range(N_KEYS // SUBLANES)])
            for shift in (4, 2, 1):
                other = [pltpu.roll(c, shift, 0) for c in col]
                col = _merge_top(col, other)
            tops.append(col)

        def pack(lists):
            out = []
            for r in range(k):
                x = lists[0][r]
                for h in range(1, PEER_HEADS):
                    x = jnp.where(sub == h, lists[h][r], x)
                out.append(x)
            return out

        a = pack([tops[2 * h] for h in range(PEER_HEADS)])
        b = pack([tops[2 * h + 1] for h in range(PEER_HEADS)])
        t = [a[0] + b[q] for q in range(k)]
        for i in range(1, k // 2):
            t = _merge_top(t, [a[i] + b[q] for q in range(k // (i + 1))])
        t = _merge_top(t, [a[i] + b[0] for i in range(k // 2, k)])
        theta = t[k - 1]
        z = jnp.ones_like(theta)
        for r in range(1, k):
            z = z + jnp.exp(t[r] - t[0])
        inv_z = 1.0 / z
        counts = []
        for r in range(k):
            cnt = jnp.zeros_like(theta)
            for q in range(k // (r + 1)):
                cnt = cnt + jnp.where(a[r] + b[q] >= theta, 1.0, 0.0)
            counts.append(cnt)

        for h in range(PEER_HEADS):
            bc = lambda x: jnp.broadcast_to(x[h:h + 1, :], (SUBLANES, LANES))
            a_h = tops[2 * h]
            b_h = tops[2 * h + 1]
            cnt_h = [bc(x) for x in counts]
            inv_z_h = bc(inv_z)
            for r in range(N_KEYS // SUBLANES):
                rs = slice(r * SUBLANES, (r + 1) * SUBLANES)
                s1 = s_ref[0, 2 * h, rs, ln]
                cnt = jnp.zeros_like(s1)
                for q in range(k):
                    cnt = jnp.where(s1 == a_h[q], cnt_h[q], cnt)
                cnt_ref[0, h, rs, ln] = _twice_bf16(cnt)
                w1_ref[0, h, rs, ln] = _twice_bf16(jnp.exp(s1 - a_h[0]) * inv_z_h)

            def second_key(start):
                s2 = s_ref[0, 2 * h + 1, pl.ds(start, SUBLANES, stride=2), ln]
                rank = jnp.full_like(s2, float(N_KEYS - 1))
                for q in range(k):
                    rank = jnp.where(s2 == b_h[q], float(q), rank)
                return rank, jnp.exp(s2 - b_h[0])

            for g in range(N_KEYS // BF16_ROWS):
                rank_even, e2_even = second_key(g * BF16_ROWS)
                rank_odd, e2_odd = second_key(g * BF16_ROWS + 1)
                ws = slice(g * SUBLANES, (g + 1) * SUBLANES)
                rank_ref[0, h, ws, ln] = _pack_pair(rank_even, rank_odd)
                e2_ref[0, h, ws, ln] = _pack_pair(e2_even, e2_odd)

    for g in range(tb // LANES):
        lane_group(slice(g * LANES, (g + 1) * LANES))


def _peer_select(st):
    tb = LANES
    b, _, _, lp = st.shape
    out = lambda rows: jax.ShapeDtypeStruct((b, PEER_HEADS, rows, lp), F32)
    spec = lambda rows: pl.BlockSpec((1, PEER_HEADS, rows, tb), lambda i, j: (i, 0, 0, j))
    return pl.pallas_call(
        _peer_select_kernel,
        grid=(b, lp // tb),
        in_specs=[pl.BlockSpec((1, 2 * PEER_HEADS, N_KEYS, tb), lambda i, j: (i, 0, 0, j))],
        out_specs=[spec(N_KEYS), spec(N_KEYS), spec(N_KEYS // 2), spec(N_KEYS // 2)],
        out_shape=[out(N_KEYS), out(N_KEYS), out(N_KEYS // 2), out(N_KEYS // 2)],
        compiler_params=_cparams(("parallel", "parallel")),
        name="peer_select",
    )(st)


def _peer_dense_kernel(ec, n2_ref, u_ref, vt_ref, rank_ref, e2_ref, cnt_ref, w1_ref, h1_ref,
                       out_ref, a_sc, w_sc, acc_sc):
    c = pl.program_id(2)
    tb = n2_ref.shape[2]

    @pl.when(c == 0)
    def _():
        acc_sc[...] = jnp.zeros_like(acc_sc)

    n2t = n2_ref[0]
    n_sub = ec // PEER_SUB

    def activations(sb):
        rows = slice(sb * PEER_SUB, (sb + 1) * PEER_SUB)
        a_sc[rows, :] = jnp.dot(u_ref[rows, :], n2t, preferred_element_type=F32)

    def gate(sb):
        il0 = sb * (PEER_SUB // N_KEYS)
        for cc in range(tb // LANES):
            ln = slice(cc * LANES, (cc + 1) * LANES)
            bc = lambda ref, h, il: pltpu.bitcast(jnp.broadcast_to(ref[0, h, il:il + 1, ln], (SUBLANES, LANES)), BF16)
            cnt = [[bc(cnt_ref, h, il0 + d) for h in range(PEER_HEADS)] for d in range(2)]
            w1 = [[bc(w1_ref, h, il0 + d) for h in range(PEER_HEADS)] for d in range(2)]
            for jp in range(N_KEYS // BF16_ROWS):
                ws = slice(jp * SUBLANES, (jp + 1) * SUBLANES)
                zero = jnp.zeros((BF16_ROWS, LANES), BF16)
                g = [zero, zero]
                for h in range(PEER_HEADS):
                    rank = pltpu.bitcast(rank_ref[0, h, ws, ln], BF16)
                    e2 = pltpu.bitcast(e2_ref[0, h, ws, ln], BF16)
                    for d in range(2):
                        g[d] = g[d] + jnp.where(rank < cnt[d][h], e2 * w1[d][h], zero)
                for d in range(2):
                    r0 = (il0 + d) * N_KEYS + jp * BF16_ROWS
                    w = g[d] * _gelu(a_sc[r0:r0 + BF16_ROWS, ln]).astype(BF16)
                    w_sc[r0 // 2:r0 // 2 + SUBLANES, ln] = pltpu.bitcast(w, F32)

    part = None
    activations(0)
    for sb in range(n_sub):
        gate(sb)
        if sb + 1 < n_sub:
            activations(sb + 1)
        rows = slice(sb * PEER_SUB, (sb + 1) * PEER_SUB)
        words = w_sc[sb * PEER_SUB // 2:(sb + 1) * PEER_SUB // 2, :]
        d = jnp.dot(vt_ref[:, rows], pltpu.bitcast(words, BF16), preferred_element_type=F32)
        part = d if part is None else part + d
    acc_sc[...] += part

    @pl.when(c == pl.num_programs(2) - 1)
    def _():
        out_ref[0] = h1_ref[0] + acc_sc[...].T


def _peer_dense(n2, u_bf, vt_bf, rank, e2, cnt, w1, h1, tb, ec):
    b, _, lp = n2.shape
    rows = ec // N_KEYS
    keyed = pl.BlockSpec((1, PEER_HEADS, N_KEYS // 2, tb), lambda i, j, c: (i, 0, 0, j))
    chunk_rows = pl.BlockSpec((1, PEER_HEADS, rows, tb), lambda i, j, c: (i, 0, c, j))
    return pl.pallas_call(
        functools.partial(_peer_dense_kernel, ec),
        grid=(b, lp // tb, N_EXPERTS // ec),
        in_specs=[pl.BlockSpec((1, D_MODEL, tb), lambda i, j, c: (i, 0, j)),
                  pl.BlockSpec((ec, D_MODEL), lambda i, j, c: (c, 0)),
                  pl.BlockSpec((D_MODEL, ec), lambda i, j, c: (0, c)),
                  keyed, keyed, chunk_rows, chunk_rows,
                  pl.BlockSpec((1, tb, D_MODEL), lambda i, j, c: (i, j, 0))],
        out_specs=pl.BlockSpec((1, tb, D_MODEL), lambda i, j, c: (i, j, 0)),
        out_shape=jax.ShapeDtypeStruct((b, lp, D_MODEL), F32),
        scratch_shapes=[pltpu.VMEM((ec, tb), F32), pltpu.VMEM((ec // 2, tb), F32), pltpu.VMEM((D_MODEL, tb), F32)],
        compiler_params=_cparams(("parallel", "parallel", "arbitrary")),
        name="peer_dense",
    )(n2, u_bf, vt_bf, rank, e2, cnt, w1, h1)


def _rope_tables(pos):
    half = ATTN_DK // 2
    inv_freq = ROPE_THETA ** (-jnp.arange(half, dtype=F32) * 2.0 / ATTN_DK)
    ang = pos[:, None] * inv_freq[None, :]
    cos = jnp.cos(ang)
    sin = jnp.sin(ang)
    return (jnp.concatenate([cos, cos, cos, cos], axis=1),
            jnp.concatenate([-sin, sin, -sin, sin], axis=1))


def _round_up(x, m):
    return (x + m - 1) // m * m


def _layer(h, l_real, pos, weights, tm, tb, ec, attend):
    (g1, w_in, qg, kg, gm, ssm_c, sg, wout, g2, wq_t, keys, u_bf, vt_bf) = weights
    cos_t, sin_t = _rope_tables(pos)
    u, k, v, qb, kb, vb = _project(h, l_real, tm, g1, w_in, qg, kg, cos_t, sin_t, gm)
    o, ys, st = attend(u, qb, kb, vb, ssm_c)
    h1, n2, sc = _finish(h, ys, o, tb, sg, wout, g2, wq_t, keys)
    cnt, w1, rank, e2 = _peer_select(sc)
    out = _peer_dense(n2, u_bf, vt_bf, rank, e2, cnt, w1, h1, tb, ec)
    return out, k, v, st


def kernel(x_prompt, x_sample, cache_k, cache_v, state_ssm_re, state_ssm_im, page_table, meta_tokens, norm1_g, w_in, q_norm_g, k_norm_g, lambda_q1, lambda_k1, lambda_q2, lambda_k2, subln_g, ssm_A_re, ssm_A_im, ssm_log_dt, ssm_B_re, ssm_B_im, ssm_C_re, ssm_C_im, ssm_D, w_glu, b_glu, w_out, norm2_g, peer_w_q, peer_keys, peer_u, peer_v):
    assert w_in.shape[0] == 1, "single-layer trunk"
    bp, seq, _ = x_prompt.shape
    db, ds, _ = x_sample.shape
    lp_real = seq + N_META
    assert lp_real % SUBLANES == 0 and ds == SUBLANES
    tm_p, t_attn, tb, ec = 768, 512, 512, 2048
    lp = _round_up(lp_real, math.lcm(tm_p, t_attn, tb))
    past = page_table.shape[1] * cache_k.shape[2]

    gm = jnp.kron(jnp.eye(ATTN_WIDTH // ATTN_DK, dtype=F32),
                  jnp.full((ATTN_DK, ATTN_DK), 1.0 / ATTN_DK, F32)).astype(BF16)
    lamp = jnp.stack([lambda_q1[0], lambda_k1[0], lambda_q2[0], lambda_k2[0]]).astype(F32)
    ssm_c = _ssm_consts(ssm_A_re[0], ssm_A_im[0], ssm_log_dt[0], ssm_B_re[0], ssm_B_im[0],
                        ssm_C_re[0], ssm_C_im[0], ssm_D[0], w_glu[0], b_glu[0])
    weights = (norm1_g[0].astype(F32)[None, :], w_in[0].astype(BF16),
               jnp.tile(q_norm_g[0].astype(F32), ATTN_WIDTH // ATTN_DK)[None, :],
               jnp.tile(k_norm_g[0].astype(F32), ATTN_WIDTH // ATTN_DK)[None, :],
               gm, ssm_c,
               jnp.tile(subln_g[0].astype(F32), N_HEADS)[None, :], w_out[0].astype(BF16),
               norm2_g[0].astype(F32)[None, :], peer_w_q[0].T.astype(BF16),
               peer_keys[0].reshape(2 * PEER_HEADS, N_KEYS, N_KEYS).astype(BF16),
               peer_u[0].astype(BF16), peer_v[0].T.astype(BF16))

    hp = jnp.concatenate([jnp.broadcast_to(meta_tokens.astype(F32)[None], (bp, N_META, D_MODEL)),
                          x_prompt.astype(F32),
                          jnp.zeros((bp, lp - lp_real, D_MODEL), F32)], axis=1)

    def attend_prompt(u, qb, kb, vb, consts):
        ys, st = _ssm(u, tm_p, lp_real - 1, consts)
        return _attn_prompt(lamp, qb, kb, vb, t_attn), ys, st

    hp_out, k_p, v_p, st_p = _layer(hp, lp_real, jnp.arange(lp, dtype=F32), weights, tm_p, tb, ec, attend_prompt)

    n_tok = db * ds
    hs = x_sample.astype(F32).reshape(1, n_tok, D_MODEL)
    h0 = jnp.concatenate([state_ssm_re[0].reshape(db, SSM_FLAT), state_ssm_im[0].reshape(db, SSM_FLAT)],
                         axis=1).astype(F32)[None]
    n_pool, page = cache_k.shape[1], cache_k.shape[2]
    ckt = jnp.transpose(cache_k[0], (0, 2, 3, 4, 1)).reshape(n_pool, ATTN_WIDTH, page)
    cv = cache_v[0].reshape(n_pool, page * N_HEADS, ATTN_DV)

    def attend_sample(u, qb, kb, vb, consts):
        ys, st = _ssm(u, 256, None, consts, h0=h0)
        seqs = lambda a: a.reshape(db, ds, a.shape[-1])
        o = _attn_sample(lamp, seqs(qb), seqs(kb), seqs(vb), ckt, cv, page_table, 16)
        return o.reshape(1, n_tok, ATTN_WIDTH), ys, st

    pos_s = jnp.tile(past + jnp.arange(ds, dtype=F32), db)
    hs_out, k_s, v_s, st_s = _layer(hs, n_tok, pos_s, weights, 512, tb, ec, attend_sample)

    y_prompt = hp_out[:, N_META:lp_real].astype(x_prompt.dtype)
    y_sample = hs_out.reshape(db, ds, D_MODEL).astype(x_sample.dtype)
    row = (lp_real - 1) % SUBLANES
    kd, vd, sd = cache_k.dtype, cache_v.dtype, state_ssm_re.dtype
    return (y_prompt, y_sample,
            k_p.reshape(1, bp, lp_real, N_HEADS, 2, ATTN_DK).astype(kd),
            v_p.reshape(1, bp, lp_real, N_HEADS, ATTN_DV).astype(vd),
            st_p[:, row, :SSM_FLAT].reshape(1, bp, SSM_GROUPS, SSM_STATE).astype(sd),
            st_p[:, row, SSM_FLAT:].reshape(1, bp, SSM_GROUPS, SSM_STATE).astype(state_ssm_im.dtype),
            k_s.reshape(1, db, ds, N_HEADS, 2, ATTN_DK).astype(kd),
            v_s.reshape(1, db, ds, N_HEADS, ATTN_DV).astype(vd),
            st_s[0, ds - 1::ds, :SSM_FLAT].reshape(1, db, SSM_GROUPS, SSM_STATE).astype(sd),
            st_s[0, ds - 1::ds, SSM_FLAT:].reshape(1, db, SSM_GROUPS, SSM_STATE).astype(state_ssm_im.dtype))
```

```python
import functools
import math

import jax
import jax.numpy as jnp
from jax import lax
from jax.experimental import pallas as pl
from jax.experimental.pallas import tpu as pltpu

F32 = jnp.float32
BF16 = jnp.bfloat16

D_MODEL = 1024
N_META = 16
SSM_WIDTH = 512
SSM_GROUP = 16
SSM_GROUPS = 32
SSM_STATE = 64
SSM_FLAT = SSM_GROUPS * SSM_STATE
ATTN_WIDTH = 512
ATTN_DV = 128
N_HEADS = 4
ATTN_DK = 64
IN_WIDTH = 2048
ROPE_THETA = 10000.0
N_KEYS = 128
N_EXPERTS = N_KEYS * N_KEYS
PEER_HEADS = 8
PEER_TOPK = 16
PEER_SUB = 8 * N_KEYS
EPS = 1e-6
NEG_BIG = -1e30
POS_BIG = 3e38
LAMBDA_INIT = 0.8 - 0.6 * math.exp(-0.3 * 0)
Q_SCALE = ATTN_DK ** -0.5 * math.log2(math.e)
ATTN_GROUP = 4

SUBLANES = 8
LANES = 128
BF16_ROWS = 2 * SUBLANES
VMEM_LIMIT = 56 * 1024 * 1024


def _cparams(sem):
    return pltpu.CompilerParams(dimension_semantics=sem, vmem_limit_bytes=VMEM_LIMIT)


GELU_C1 = 0.7978845608028654
GELU_C2 = GELU_C1 * 0.044715


def _gelu(x):
    return x * (0.5 + 0.5 * jnp.tanh(x * (GELU_C1 + GELU_C2 * (x * x))))


def _proj_kernel(x_ref, g1_ref, w_ref, qg_ref, kg_ref, cos_ref, sin_ref, gm_ref,
                 u_ref, k_ref, v_ref, qb_ref, kb_ref, vb_ref):
    x = x_ref[0]
    tm = x.shape[0]
    ms = jnp.mean(x * x, axis=-1, keepdims=True)
    n = (x * lax.rsqrt(ms + EPS) * g1_ref[...]).astype(BF16)
    z = jnp.dot(n, w_ref[...], preferred_element_type=F32)
    u_ref[0] = z[:, :SSM_WIDTH]
    v = z[:, SSM_WIDTH + 2 * ATTN_WIDTH:]
    v_ref[0] = v
    vb = v.astype(BF16)
    ones = jnp.ones((tm, ATTN_DV), BF16)
    vb_ref[0] = jnp.concatenate([t for h in range(N_HEADS) for t in (vb[:, h * ATTN_DV:(h + 1) * ATTN_DV], ones)], axis=1)

    cos = jnp.concatenate([cos_ref[...]] * 4, axis=1)
    sin = jnp.concatenate([sin_ref[...]] * 4, axis=1)
    lane = lax.broadcasted_iota(jnp.int32, (tm, ATTN_WIDTH), 1)
    first_half = (lane & (ATTN_DK - 1)) < (ATTN_DK // 2)
    gm = gm_ref[...]

    def norm_rope(t, g):
        sq = t * t
        hi = sq.astype(BF16)
        lo = (sq - hi.astype(F32)).astype(BF16)
        ms64 = (jnp.dot(hi, gm, preferred_element_type=F32)
                + jnp.dot(lo, gm, preferred_element_type=F32))
        tn = t * lax.rsqrt(ms64 + EPS) * g
        swapped = jnp.where(first_half,
                            pltpu.roll(tn, ATTN_WIDTH - ATTN_DK // 2, 1),
                            pltpu.roll(tn, ATTN_DK // 2, 1))
        return tn * cos + swapped * sin

    q = norm_rope(z[:, SSM_WIDTH:SSM_WIDTH + ATTN_WIDTH], qg_ref[...])
    k = norm_rope(z[:, SSM_WIDTH + ATTN_WIDTH:SSM_WIDTH + 2 * ATTN_WIDTH], kg_ref[...])
    k_ref[0] = k
    kb_ref[0] = k.astype(BF16)
    qb_ref[0] = (q * Q_SCALE).astype(BF16)


def _project(h, l_out, tm, g1, w_bf, qg, kg, cos_t, sin_t, gm):
    b, lp, _ = h.shape
    nblk = lp // tm
    tok = lambda w: pl.BlockSpec((1, tm, w), lambda i, j: (i, j, 0))
    full = lambda a: pl.BlockSpec(a.shape, lambda i, j: (0,) * a.ndim)
    return pl.pallas_call(
        _proj_kernel,
        grid=(b, nblk),
        in_specs=[tok(D_MODEL), full(g1), full(w_bf), full(qg), full(kg),
                  pl.BlockSpec((tm, LANES), lambda i, j: (j, 0)),
                  pl.BlockSpec((tm, LANES), lambda i, j: (j, 0)), full(gm)],
        out_specs=[tok(SSM_WIDTH), tok(ATTN_WIDTH), tok(ATTN_WIDTH),
                   tok(ATTN_WIDTH), tok(ATTN_WIDTH), tok(2 * ATTN_WIDTH)],
        out_shape=[jax.ShapeDtypeStruct((b, lp, SSM_WIDTH), F32),
                   jax.ShapeDtypeStruct((b, l_out, ATTN_WIDTH), F32),
                   jax.ShapeDtypeStruct((b, l_out, ATTN_WIDTH), F32),
                   jax.ShapeDtypeStruct((b, lp, ATTN_WIDTH), BF16),
                   jax.ShapeDtypeStruct((b, lp, ATTN_WIDTH), BF16),
                   jax.ShapeDtypeStruct((b, lp, 2 * ATTN_WIDTH), BF16)],
        compiler_params=_cparams(("parallel", "parallel")),
        name="project",
    )(h, g1, w_bf, qg, kg, cos_t, sin_t, gm)


SSM_CHUNK = 512


def _ssm_kernel(sequential, state_row, state_blk, *refs):
    if sequential:
        (u_ref, bbd_ref, cbd_ref, kc_ref, pc_ref, d_ref, wg_ref, bg_ref,
         y_ref, st_ref, x_sc, carry_sc) = refs
        h0_ref = None
    else:
        (u_ref, bbd_ref, cbd_ref, kc_ref, pc_ref, d_ref, wg_ref, bg_ref, h0_ref,
         y_ref, st_ref, x_sc) = refs
        carry_sc = None
    j = pl.program_id(1)
    u = u_ref[0]
    tm = u.shape[0]
    ub = u.astype(BF16)
    hw, hf = SSM_WIDTH // 2, SSM_FLAT // 2
    for half in range(2):
        xh = jnp.dot(ub[:, half * hw:(half + 1) * hw], bbd_ref[half], preferred_element_type=F32)
        x_sc[:, half * hf:(half + 1) * hf] = xh[:, :hf]
        x_sc[:, SSM_FLAT + half * hf:SSM_FLAT + (half + 1) * hf] = xh[:, hf:]

    if sequential:
        @pl.when(j == 0)
        def _():
            carry_sc[...] = jnp.zeros_like(carry_sc)

    def tile_body(i, carry):
        r0 = pl.multiple_of(i * SUBLANES, SUBLANES)
        for c in range(0, SSM_FLAT, SSM_CHUNK):
            re = slice(c, c + SSM_CHUNK)
            im = slice(SSM_FLAT + c, SSM_FLAT + c + SSM_CHUNK)
            xr = x_sc[pl.ds(r0, SUBLANES), re]
            xi = x_sc[pl.ds(r0, SUBLANES), im]
            for di, d in enumerate((1, 2, 4)):
                cr = kc_ref[di, 0, :, re]
                ci = kc_ref[di, 1, :, re]
                rr = pltpu.roll(xr, d, 0)
                ri = pltpu.roll(xi, d, 0)
                xr, xi = xr + cr * rr - ci * ri, xi + cr * ri + ci * rr
            if sequential:
                car_r = carry_sc[:, re]
                car_i = carry_sc[:, im]
            else:
                car_r = h0_ref[0, pl.ds(r0, SUBLANES), re]
                car_i = h0_ref[0, pl.ds(r0, SUBLANES), im]
            pr = pc_ref[0, :, re]
            pi = pc_ref[1, :, re]
            xr, xi = xr + pr * car_r - pi * car_i, xi + pr * car_i + pi * car_r
            x_sc[pl.ds(r0, SUBLANES), re] = xr
            x_sc[pl.ds(r0, SUBLANES), im] = xi
            if sequential:
                carry_sc[:, re] = jnp.broadcast_to(xr[SUBLANES - 1:, :], (SUBLANES, SSM_CHUNK))
                carry_sc[:, im] = jnp.broadcast_to(xi[SUBLANES - 1:, :], (SUBLANES, SSM_CHUNK))
        return carry

    lax.fori_loop(0, tm // SUBLANES, tile_body, 0)

    if sequential:
        @pl.when(j == state_blk)
        def _():
            st_ref[0] = x_sc[state_row:state_row + SUBLANES, :]
    else:
        st_ref[0] = x_sc[...]

    ys = []
    for half in range(2):
        x_re = x_sc[:, half * hf:(half + 1) * hf].astype(BF16)
        x_im = x_sc[:, SSM_FLAT + half * hf:SSM_FLAT + (half + 1) * hf].astype(BF16)
        ys.append(jnp.dot(x_re, cbd_ref[half, :hf], preferred_element_type=F32)
                  + jnp.dot(x_im, cbd_ref[half, hf:], preferred_element_type=F32))
    y = jnp.concatenate(ys, axis=1) + d_ref[...] * u
    y = _gelu(y)
    gate = jnp.dot(y.astype(BF16), wg_ref[...], preferred_element_type=F32) + bg_ref[...]
    y_ref[0] = (y * (1.0 / (1.0 + jnp.exp(-gate)))).astype(BF16)


def _ssm(u, tm, last_token, consts, h0=None):
    bbd, cbd, kc, pc, dsk, wg, bg = consts
    b, lp, _ = u.shape
    nblk = lp // tm
    sequential = h0 is None
    full = lambda a: pl.BlockSpec(a.shape, lambda i, j: (0,) * a.ndim)
    in_specs = [pl.BlockSpec((1, tm, SSM_WIDTH), lambda i, j: (i, j, 0)),
                full(bbd), full(cbd), full(kc), full(pc), full(dsk), full(wg), full(bg)]
    args = [u, bbd, cbd, kc, pc, dsk, wg, bg]
    scratch = [pltpu.VMEM((tm, 2 * SSM_FLAT), F32)]
    if sequential:
        state_blk = last_token // tm
        state_row = (last_token % tm) // SUBLANES * SUBLANES
        st_spec = pl.BlockSpec((1, SUBLANES, 2 * SSM_FLAT), lambda i, j: (i, 0, 0))
        st_shape = jax.ShapeDtypeStruct((b, SUBLANES, 2 * SSM_FLAT), F32)
        scratch.append(pltpu.VMEM((SUBLANES, 2 * SSM_FLAT), F32))
        sem = ("parallel", "arbitrary")
    else:
        state_blk = state_row = 0
        in_specs.append(pl.BlockSpec((1, tm, 2 * SSM_FLAT), lambda i, j: (i, j, 0)))
        args.append(jnp.repeat(h0, SUBLANES, axis=1))
        st_spec = pl.BlockSpec((1, tm, 2 * SSM_FLAT), lambda i, j: (i, j, 0))
        st_shape = jax.ShapeDtypeStruct((b, lp, 2 * SSM_FLAT), F32)
        sem = ("parallel", "parallel")
    return pl.pallas_call(
        functools.partial(_ssm_kernel, sequential, state_row, state_blk),
        grid=(b, nblk),
        in_specs=in_specs,
        out_specs=[pl.BlockSpec((1, tm, SSM_WIDTH), lambda i, j: (i, j, 0)), st_spec],
        out_shape=[jax.ShapeDtypeStruct((b, lp, SSM_WIDTH), BF16), st_shape],
        scratch_shapes=scratch,
        compiler_params=_cparams(sem),
        name="s5_prompt" if sequential else "s5_sample",
    )(*args)


def _ssm_consts(a_re, a_im, log_dt, b_re, b_im, c_re, c_im, d_skip, w_glu, b_glu):
    dt = jnp.exp(log_dt.astype(F32))[:, None]
    a_re = a_re.astype(F32)
    a_im = a_im.astype(F32)
    mag = jnp.exp(dt * a_re)
    ab_re = mag * jnp.cos(dt * a_im)
    ab_im = mag * jnp.sin(dt * a_im)
    den = a_re * a_re + a_im * a_im
    z_re = ((ab_re - 1.0) * a_re + ab_im * a_im) / den
    z_im = (ab_im * a_re - (ab_re - 1.0) * a_im) / den
    b_re = b_re.astype(F32)
    b_im = b_im.astype(F32)
    bb_re = z_re[..., None] * b_re - z_im[..., None] * b_im
    bb_im = z_re[..., None] * b_im + z_im[..., None] * b_re
    eye = jnp.eye(SSM_GROUPS, dtype=F32)
    bbd = jnp.concatenate(
        [jnp.einsum('gnc,gh->gchn', bb_re, eye).reshape(SSM_WIDTH, SSM_FLAT),
         jnp.einsum('gnc,gh->gchn', bb_im, eye).reshape(SSM_WIDTH, SSM_FLAT)], axis=1).astype(BF16)
    cbd = jnp.concatenate(
        [jnp.einsum('gcn,gh->gnhc', c_re.astype(F32), eye).reshape(SSM_FLAT, SSM_WIDTH),
         -jnp.einsum('gcn,gh->gnhc', c_im.astype(F32), eye).reshape(SSM_FLAT, SSM_WIDTH)], axis=0).astype(BF16)
    hw, hf = SSM_WIDTH // 2, SSM_FLAT // 2
    half_b = lambda k: jnp.concatenate([bbd[k * hw:(k + 1) * hw, k * hf:(k + 1) * hf],
                                        bbd[k * hw:(k + 1) * hw, SSM_FLAT + k * hf:SSM_FLAT + (k + 1) * hf]], axis=1)
    half_c = lambda k: jnp.concatenate([cbd[k * hf:(k + 1) * hf, k * hw:(k + 1) * hw],
                                        cbd[SSM_FLAT + k * hf:SSM_FLAT + (k + 1) * hf, k * hw:(k + 1) * hw]], axis=0)
    bbd = jnp.stack([half_b(0), half_b(1)])
    cbd = jnp.stack([half_c(0), half_c(1)])

    def power(p):
        m = jnp.exp(p * dt * a_re)
        return (m * jnp.cos(p * dt * a_im)).reshape(-1), (m * jnp.sin(p * dt * a_im)).reshape(-1)

    row = jnp.arange(SUBLANES)[:, None]
    kc = []
    for d in (1, 2, 4):
        pr, pi = power(float(d))
        kc.append(jnp.stack([jnp.where(row >= d, pr[None, :], 0.0), jnp.where(row >= d, pi[None, :], 0.0)]))
    kc = jnp.stack(kc)
    rows = [power(float(s + 1)) for s in range(SUBLANES)]
    pc = jnp.stack([jnp.stack([r[0] for r in rows]), jnp.stack([r[1] for r in rows])])
    return (bbd, cbd, kc, pc, d_skip.astype(F32)[None, :], w_glu.astype(BF16), b_glu.astype(F32)[None, :])


def _lambda_value(lamp_ref):
    lp = lamp_ref[...]
    s1 = jnp.sum(lp[0:1] * lp[1:2], axis=1, keepdims=True)
    s2 = jnp.sum(lp[2:3] * lp[3:4], axis=1, keepdims=True)
    return jnp.exp(s1) - jnp.exp(s2) + LAMBDA_INIT


def _stack_sub_queries(q):
    lane = lax.broadcasted_iota(jnp.int32, q.shape, 1)
    zero = jnp.zeros_like(q)
    return jnp.concatenate([jnp.where(lane < ATTN_DK, q, zero), jnp.where(lane >= ATTN_DK, q, zero)], axis=0)


def _lane_tile(x, n):
    return x if n == 1 else jnp.concatenate([x] * n, axis=1)


def _flash_update(s, v, m_sc, acc_sc):
    tk = s.shape[1]
    m_prev = m_sc[...]
    m_new = jnp.maximum(m_prev, jnp.max(s, axis=1, keepdims=True))
    p = jnp.exp2(s - _lane_tile(m_new, tk // LANES))
    alpha = jnp.exp2(m_prev - m_new)
    acc_sc[...] = _lane_tile(alpha, 2) * acc_sc[...] + jnp.dot(p.astype(BF16), v, preferred_element_type=F32)
    m_sc[...] = m_new


def _attn_prompt_kernel(t, lamp_ref, q_ref, k_ref, v_ref, o_ref, q2_sc, m_sc, acc_sc):
    iq = pl.program_id(2)
    q2_sc[...] = _stack_sub_queries(q_ref[0])
    m_sc[...] = jnp.full_like(m_sc, NEG_BIG)
    acc_sc[...] = jnp.zeros_like(acc_sc)

    def scores(ik):
        k0 = pl.multiple_of(ik * t, t)
        s = lax.dot_general(q2_sc[...], k_ref[0, pl.ds(k0, t), :], (((1,), (1,)), ((), ())),
                            preferred_element_type=F32)
        return s, v_ref[0, pl.ds(k0, t), :]

    def causal(s):
        row = lax.broadcasted_iota(jnp.int32, s.shape, 0)
        col = lax.broadcasted_iota(jnp.int32, s.shape, 1)
        return jnp.where(col <= jnp.where(row >= t, row - t, row), s, NEG_BIG)

    def tiles(indices, diag_last):
        sv = [scores(i) for i in indices]
        for n, (s, v) in enumerate(sv):
            _flash_update(causal(s) if diag_last and n == len(sv) - 1 else s, v, m_sc, acc_sc)

    def group_body(i, c):
        tiles([ATTN_GROUP * i + n for n in range(ATTN_GROUP)], False)
        return c

    lax.fori_loop(0, iq // ATTN_GROUP, group_body, 0)

    for rest in range(ATTN_GROUP):
        @pl.when(iq % ATTN_GROUP == rest)
        def _():
            tiles([iq - rest + n for n in range(rest)] + [iq], True)

    lam = _lambda_value(lamp_ref)
    acc = acc_sc[...]
    o = acc[:, :ATTN_DV] / acc[:, ATTN_DV:]
    o_ref[0] = o[:t] - lam * o[t:]


def _attn_prompt(lamp, qb, kb, vb, t):
    b, lp, _ = qb.shape
    return pl.pallas_call(
        functools.partial(_attn_prompt_kernel, t),
        grid=(b, N_HEADS, lp // t),
        in_specs=[pl.BlockSpec(lamp.shape, lambda i, h, j: (0, 0)),
                  pl.BlockSpec((1, t, ATTN_DV), lambda i, h, j: (i, j, h)),
                  pl.BlockSpec((1, lp, ATTN_DV), lambda i, h, j: (i, 0, h)),
                  pl.BlockSpec((1, lp, 2 * ATTN_DV), lambda i, h, j: (i, 0, h))],
        out_specs=pl.BlockSpec((1, t, ATTN_DV), lambda i, h, j: (i, j, h)),
        out_shape=jax.ShapeDtypeStruct((b, lp, ATTN_WIDTH), F32),
        scratch_shapes=[pltpu.VMEM((2 * t, ATTN_DV), BF16),
                        pltpu.VMEM((2 * t, LANES), F32),
                        pltpu.VMEM((2 * t, 2 * ATTN_DV), F32)],
        compiler_params=_cparams(("parallel", "parallel", "parallel")),
        name="attn_prompt",
    )(lamp, qb, kb, vb)


def _attn_sample_kernel(n_pages, s_len, pt_ref, lamp_ref, q_ref, kn_ref, vn_ref, *refs):
    kt_refs = refs[:n_pages]
    v_refs = refs[n_pages:2 * n_pages]
    o_ref, q2_sc, m_sc, l_sc, acc_sc = refs[2 * n_pages:]
    j = pl.program_id(1)
    hr = 2 * s_len

    @pl.when(j == 0)
    def _():
        q = q_ref[0]
        lane = lax.broadcasted_iota(jnp.int32, q.shape, 1)
        zero = jnp.zeros_like(q)
        q2_sc[...] = jnp.concatenate(
            [jnp.where((lane >= lo) & (lane < lo + ATTN_DK), q, zero) for lo in range(0, ATTN_WIDTH, ATTN_DK)], axis=0)
        m_sc[...] = jnp.full_like(m_sc, NEG_BIG)
        l_sc[...] = jnp.zeros_like(l_sc)
        acc_sc[...] = jnp.zeros_like(acc_sc)

    def update(s, values):
        tk = s.shape[1]
        m_prev = m_sc[...]
        m_new = jnp.maximum(m_prev, jnp.max(s, axis=1, keepdims=True))
        p = jnp.exp2(s - _lane_tile(m_new, tk // LANES))
        alpha = jnp.exp2(m_prev - m_new)
        l_sc[...] = alpha * l_sc[...] + jnp.sum(p, axis=1, keepdims=True)
        pb = p.astype(BF16)
        for h in range(N_HEADS):
            rs = slice(h * hr, (h + 1) * hr)
            acc_sc[rs, :] = alpha[rs] * acc_sc[rs, :] + jnp.dot(pb[rs], values(h), preferred_element_type=F32)
        m_sc[...] = m_new

    kt = jnp.concatenate([r[0] for r in kt_refs], axis=1).astype(BF16)
    s = jnp.dot(q2_sc[...], kt, preferred_element_type=F32)
    update(s, lambda h: jnp.concatenate([r[0, pl.ds(h, LANES, stride=N_HEADS), :] for r in v_refs],
                                        axis=0).astype(BF16))

    @pl.when(j == pl.num_programs(1) - 1)
    def _():
        lam = _lambda_value(lamp_ref)
        pad_k = jnp.zeros((LANES - s_len, ATTN_WIDTH), BF16)
        pad_v = jnp.zeros((LANES - s_len, ATTN_DV), BF16)
        s_own = lax.dot_general(q2_sc[...], jnp.concatenate([kn_ref[0], pad_k], axis=0),
                                (((1,), (1,)), ((), ())), preferred_element_type=F32)
        row = lax.broadcasted_iota(jnp.int32, s_own.shape, 0)
        col = lax.broadcasted_iota(jnp.int32, s_own.shape, 1)
        s_own = jnp.where(col <= (row & (s_len - 1)), s_own, NEG_BIG)
        update(s_own, lambda h: jnp.concatenate([vn_ref[0, :, 2 * h * ATTN_DV:(2 * h + 1) * ATTN_DV], pad_v], axis=0))
        o = acc_sc[...] / l_sc[...]
        for h in range(N_HEADS):
            o_ref[0, :, h * ATTN_DV:(h + 1) * ATTN_DV] = (o[h * hr:h * hr + s_len]
                                                          - lam * o[h * hr + s_len:(h + 1) * hr])


def _attn_sample(lamp, qb, kb, vb, cache_kt, cache_v, page_table, n_pages):
    db, s_len, _ = qb.shape
    n_past_pages = page_table.shape[1]
    assert cache_kt.shape[2] == LANES and s_len & (s_len - 1) == 0 and n_past_pages % n_pages == 0
    steps = n_past_pages // n_pages
    tok = lambda w: pl.BlockSpec((1, s_len, w), lambda i, j, pt: (i, 0, 0))

    def page_spec(p):
        return pl.BlockSpec((1, ATTN_WIDTH, LANES), lambda i, j, pt: (pt[i, j * n_pages + p], 0, 0))

    rows = 2 * s_len * N_HEADS
    return pl.pallas_call(
        functools.partial(_attn_sample_kernel, n_pages, s_len),
        grid_spec=pltpu.PrefetchScalarGridSpec(
            num_scalar_prefetch=1,
            grid=(db, steps),
            in_specs=[pl.BlockSpec(lamp.shape, lambda i, j, pt: (0, 0)),
                      tok(ATTN_WIDTH), tok(ATTN_WIDTH), tok(2 * ATTN_WIDTH)]
                     + [page_spec(p) for p in range(n_pages)] * 2,
            out_specs=tok(ATTN_WIDTH),
            scratch_shapes=[pltpu.VMEM((rows, ATTN_WIDTH), BF16),
                            pltpu.VMEM((rows, LANES), F32),
                            pltpu.VMEM((rows, LANES), F32),
                            pltpu.VMEM((rows, ATTN_DV), F32)]),
        out_shape=jax.ShapeDtypeStruct((db, s_len, ATTN_WIDTH), F32),
        compiler_params=_cparams(("parallel", "arbitrary")),
        name="attn_sample",
    )(page_table, lamp, qb, kb, vb, *([cache_kt] * n_pages), *([cache_v] * n_pages))


def _finish_kernel(h_ref, ys_ref, o_ref, sg_ref, wout_ref, g2_ref, wq_ref, keys_ref,
                   h1_ref, n2_ref, st_ref):
    o = o_ref[0]
    parts = []
    for h in range(N_HEADS):
        oh = o[:, h * ATTN_DV:(h + 1) * ATTN_DV]
        parts.append(oh * lax.rsqrt(jnp.mean(oh * oh, axis=-1, keepdims=True) + EPS))
    on = jnp.concatenate(parts, axis=1) * sg_ref[...] * (1.0 - LAMBDA_INIT)
    mix = jnp.concatenate([ys_ref[0], on.astype(BF16)], axis=1)
    h1 = h_ref[0] + jnp.dot(mix, wout_ref[...], preferred_element_type=F32)
    h1_ref[0] = h1
    n2 = (h1 * lax.rsqrt(jnp.mean(h1 * h1, axis=-1, keepdims=True) + EPS) * g2_ref[...]).astype(BF16)
    n2_ref[0] = n2
    qt = lax.dot_general(wq_ref[...], n2, (((1,), (1,)), ((), ())), preferred_element_type=F32)
    for hp in range(2 * PEER_HEADS):
        st_ref[0, hp] = jnp.dot(keys_ref[hp], qt[hp * N_KEYS:(hp + 1) * N_KEYS].astype(BF16),
                                preferred_element_type=F32)


def _finish(h, ys, o, tm, sg, wout, g2, wq_t, keys):
    b, lp, _ = h.shape
    nblk = lp // tm
    tok = lambda w: pl.BlockSpec((1, tm, w), lambda i, j: (i, j, 0))
    full = lambda a: pl.BlockSpec(a.shape, lambda i, j: (0,) * a.ndim)
    return pl.pallas_call(
        _finish_kernel,
        grid=(b, nblk),
        in_specs=[tok(D_MODEL), tok(SSM_WIDTH), tok(ATTN_WIDTH), full(sg), full(wout), full(g2),
                  full(wq_t), full(keys)],
        out_specs=[tok(D_MODEL), tok(D_MODEL),
                   pl.BlockSpec((1, 2 * PEER_HEADS, N_KEYS, tm), lambda i, j: (i, 0, 0, j))],
        out_shape=[jax.ShapeDtypeStruct((b, lp, D_MODEL), F32),
                   jax.ShapeDtypeStruct((b, lp, D_MODEL), BF16),
                   jax.ShapeDtypeStruct((b, 2 * PEER_HEADS, N_KEYS, lp), F32)],
        compiler_params=_cparams(("parallel", "parallel")),
        name="finish",
    )(h, ys, o, sg, wout, g2, wq_t, keys)


def _cmpx(v, i, j):
    hi = jnp.maximum(v[i], v[j])
    lo = jnp.minimum(v[i], v[j])
    v[i], v[j] = hi, lo


def _bitonic_merge_desc(v):
    n = len(v)
    v = list(v)
    d = n // 2
    while d >= 1:
        for i in range(n):
            if (i % (2 * d)) < d:
                _cmpx(v, i, i + d)
        d //= 2
    return v


def _sort_desc(v):
    n = len(v)
    if n == 1:
        return list(v)
    a = _sort_desc(v[:n // 2])
    b = _sort_desc(v[n // 2:])
    return _bitonic_merge_desc(a + b[::-1])


def _merge_top(t, s):
    n = len(t)
    m = list(t)
    for r in range(n):
        q = n - 1 - r
        if q < len(s):
            m[r] = jnp.maximum(t[r], s[q])
    return _bitonic_merge_desc(m)


def _pack_pair(lo, hi):
    lo_bits = lax.bitcast_convert_type(lo.astype(BF16).astype(F32), jnp.uint32)
    hi_bits = lax.bitcast_convert_type(hi.astype(BF16).astype(F32), jnp.uint32)
    return lax.bitcast_convert_type(hi_bits | (lo_bits >> 16), F32)


def _twice_bf16(x):
    return _pack_pair(x, x)


def _peer_select_kernel(s_ref, cnt_ref, w1_ref, rank_ref, e2_ref):
    tb = s_ref.shape[-1]
    sub = lax.broadcasted_iota(jnp.int32, (SUBLANES, LANES), 0)
    k = PEER_TOPK

    def lane_group(ln):
        tops = []
        for hp in range(2 * PEER_HEADS):
            col = _sort_desc([s_ref[0, hp, r * SUBLANES:(r + 1) * SUBLANES, ln] for r in range(N_KEYS // SUBLANES)])
            for shift in (4, 2, 1):
                other = [pltpu.roll(c, shift, 0) for c in col]
                col = _merge_top(col, other)
            tops.append(col)

        def pack(lists):
            out = []
            for r in range(k):
                x = lists[0][r]
                for h in range(1, PEER_HEADS):
                    x = jnp.where(sub == h, lists[h][r], x)
                out.append(x)
            return out

        a = pack([tops[2 * h] for h in range(PEER_HEADS)])
        b = pack([tops[2 * h + 1] for h in range(PEER_HEADS)])
        t = [a[0] + b[q] for q in range(k)]
        for i in range(1, k // 2):
            t = _merge_top(t, [a[i] + b[q] for q in range(k // (i + 1))])
        t = _merge_top(t, [a[i] + b[0] for i in range(k // 2, k)])
        theta = t[k - 1]
        z = jnp.ones_like(theta)
        for r in range(1, k):
            z = z + jnp.exp(t[r] - t[0])
        inv_z = 1.0 / z
        counts = []
        for r in range(k):
            cnt = jnp.zeros_like(theta)
            for q in range(k // (r + 1)):
                cnt = cnt + jnp.where(a[r] + b[q] >= theta, 1.0, 0.0)
            counts.append(cnt)

        for h in range(PEER_HEADS):
            bc = lambda x: jnp.broadcast_to(x[h:h + 1, :], (SUBLANES, LANES))
            a_h = tops[2 * h]
            b_h = tops[2 * h + 1]
            cnt_h = [bc(x) for x in counts]
            inv_z_h = bc(inv_z)
            for r in range(N_KEYS // SUBLANES):
                rs = slice(r * SUBLANES, (r + 1) * SUBLANES)
                s1 = s_ref[0, 2 * h, rs, ln]
                cnt = jnp.zeros_like(s1)
                for q in range(k):
                    cnt = jnp.where(s1 == a_h[q], cnt_h[q], cnt)
                cnt_ref[0, h, rs, ln] = _twice_bf16(cnt)
                w1_ref[0, h, rs, ln] = _twice_bf16(jnp.exp(s1 - a_h[0]) * inv_z_h)

            def second_key(start):
                s2 = s_ref[0, 2 * h + 1, pl.ds(start, SUBLANES, stride=2), ln]
                rank = jnp.full_like(s2, float(N_KEYS - 1))
                for q in range(k):
                    rank = jnp.where(s2 == b_h[q], float(q), rank)
                return rank, jnp.exp(s2 - b_h[0])

            for g in range(N_KEYS // BF16_ROWS):
                rank_even, e2_even = second_key(g * BF16_ROWS)
                rank_odd, e2_odd = second_key(g * BF16_ROWS + 1)
                ws = slice(g * SUBLANES, (g + 1) * SUBLANES)
                rank_ref[0, h, ws, ln] = _pack_pair(rank_even, rank_odd)
                e2_ref[0, h, ws, ln] = _pack_pair(e2_even, e2_odd)

    for g in range(tb // LANES):
        lane_group(slice(g * LANES, (g + 1) * LANES))


def _peer_select(st):
    tb = LANES
    b, _, _, lp = st.shape
    out = lambda rows: jax.ShapeDtypeStruct((b, PEER_HEADS, rows, lp), F32)
    spec = lambda rows: pl.BlockSpec((1, PEER_HEADS, rows, tb), lambda i, j: (i, 0, 0, j))
    return pl.pallas_call(
        _peer_select_kernel,
        grid=(b, lp // tb),
        in_specs=[pl.BlockSpec((1, 2 * PEER_HEADS, N_KEYS, tb), lambda i, j: (i, 0, 0, j))],
        out_specs=[spec(N_KEYS), spec(N_KEYS), spec(N_KEYS // 2), spec(N_KEYS // 2)],
        out_shape=[out(N_KEYS), out(N_KEYS), out(N_KEYS // 2), out(N_KEYS // 2)],
        compiler_params=_cparams(("parallel", "parallel")),
        name="peer_select",
    )(st)


def _peer_dense_kernel(ec, n2_ref, u_ref, vt_ref, rank_ref, e2_ref, cnt_ref, w1_ref, h1_ref,
                       out_ref, a_sc, w_sc, acc_sc):
    c = pl.program_id(2)
    tb = n2_ref.shape[1]

    @pl.when(c == 0)
    def _():
        acc_sc[...] = jnp.zeros_like(acc_sc)

    n2 = n2_ref[0]
    n_sub = ec // PEER_SUB

    def activations(sb):
        rows = slice(sb * PEER_SUB, (sb + 1) * PEER_SUB)
        a_sc[rows, :] = lax.dot_general(u_ref[rows, :], n2, (((1,), (1,)), ((), ())), preferred_element_type=F32)

    def gate(sb):
        for pair in range(PEER_SUB // (2 * N_KEYS)):
            gate_pair(sb * (PEER_SUB // N_KEYS) + 2 * pair)

    def gate_pair(il0):
        for cc in range(tb // LANES):
            ln = slice(cc * LANES, (cc + 1) * LANES)
            bc = lambda ref, h, il: pltpu.bitcast(jnp.broadcast_to(ref[0, h, il:il + 1, ln], (SUBLANES, LANES)), BF16)
            cnt = [[bc(cnt_ref, h, il0 + d) for h in range(PEER_HEADS)] for d in range(2)]
            w1 = [[bc(w1_ref, h, il0 + d) for h in range(PEER_HEADS)] for d in range(2)]
            for jp in range(N_KEYS // BF16_ROWS):
                ws = slice(jp * SUBLANES, (jp + 1) * SUBLANES)
                zero = jnp.zeros((BF16_ROWS, LANES), BF16)
                g = [zero, zero]
                for h in range(PEER_HEADS):
                    rank = pltpu.bitcast(rank_ref[0, h, ws, ln], BF16)
                    e2 = pltpu.bitcast(e2_ref[0, h, ws, ln], BF16)
                    for d in range(2):
                        g[d] = g[d] + jnp.where(rank < cnt[d][h], e2 * w1[d][h], zero)
                for d in range(2):
                    r0 = (il0 + d) * N_KEYS + jp * BF16_ROWS
                    w = g[d] * _gelu(a_sc[r0:r0 + BF16_ROWS, ln]).astype(BF16)
                    w_sc[r0 // 2:r0 // 2 + SUBLANES, ln] = pltpu.bitcast(w, F32)

    part = None
    activations(0)
    for sb in range(n_sub):
        gate(sb)
        if sb + 1 < n_sub:
            activations(sb + 1)
        rows = slice(sb * PEER_SUB, (sb + 1) * PEER_SUB)
        words = w_sc[sb * PEER_SUB // 2:(sb + 1) * PEER_SUB // 2, :]
        d = jnp.dot(vt_ref[:, rows], pltpu.bitcast(words, BF16), preferred_element_type=F32)
        part = d if part is None else part + d
    acc_sc[...] += part

    @pl.when(c == pl.num_programs(2) - 1)
    def _():
        out_ref[0] = h1_ref[0] + acc_sc[...].T


def _peer_dense(n2, u_bf, vt_bf, rank, e2, cnt, w1, h1, tb, ec):
    b, lp, _ = n2.shape
    rows = ec // N_KEYS
    keyed = pl.BlockSpec((1, PEER_HEADS, N_KEYS // 2, tb), lambda i, j, c: (i, 0, 0, j))
    chunk_rows = pl.BlockSpec((1, PEER_HEADS, rows, tb), lambda i, j, c: (i, 0, c, j))
    return pl.pallas_call(
        functools.partial(_peer_dense_kernel, ec),
        grid=(b, lp // tb, N_EXPERTS // ec),
        in_specs=[pl.BlockSpec((1, tb, D_MODEL), lambda i, j, c: (i, j, 0)),
                  pl.BlockSpec((ec, D_MODEL), lambda i, j, c: (c, 0)),
                  pl.BlockSpec((D_MODEL, ec), lambda i, j, c: (0, c)),
                  keyed, keyed, chunk_rows, chunk_rows,
                  pl.BlockSpec((1, tb, D_MODEL), lambda i, j, c: (i, j, 0))],
        out_specs=pl.BlockSpec((1, tb, D_MODEL), lambda i, j, c: (i, j, 0)),
        out_shape=jax.ShapeDtypeStruct((b, lp, D_MODEL), F32),
        scratch_shapes=[pltpu.VMEM((ec, tb), F32), pltpu.VMEM((ec // 2, tb), F32), pltpu.VMEM((D_MODEL, tb), F32)],
        compiler_params=_cparams(("parallel", "parallel", "arbitrary")),
        name="peer_dense",
    )(n2, u_bf, vt_bf, rank, e2, cnt, w1, h1)


def _rope_tables(pos):
    half = ATTN_DK // 2
    inv_freq = ROPE_THETA ** (-jnp.arange(half, dtype=F32) * 2.0 / ATTN_DK)
    ang = pos[:, None] * inv_freq[None, :]
    cos = jnp.cos(ang)
    sin = jnp.sin(ang)
    return (jnp.concatenate([cos, cos, cos, cos], axis=1),
            jnp.concatenate([-sin, sin, -sin, sin], axis=1))


def _round_up(x, m):
    return (x + m - 1) // m * m


def _layer(h, l_real, pos, weights, tm, tb, ec, attend):
    (g1, w_in, qg, kg, gm, ssm_c, sg, wout, g2, wq_t, keys, u_bf, vt_bf) = weights
    cos_t, sin_t = _rope_tables(pos)
    u, k, v, qb, kb, vb = _project(h, l_real, tm, g1, w_in, qg, kg, cos_t, sin_t, gm)
    o, ys, st = attend(u, qb, kb, vb, ssm_c)
    h1, n2, sc = _finish(h, ys, o, tb, sg, wout, g2, wq_t, keys)
    cnt, w1, rank, e2 = _peer_select(sc)
    out = _peer_dense(n2, u_bf, vt_bf, rank, e2, cnt, w1, h1, tb, ec)
    return out, k, v, st


def kernel(x_prompt, x_sample, cache_k, cache_v, state_ssm_re, state_ssm_im, page_table, meta_tokens, norm1_g, w_in, q_norm_g, k_norm_g, lambda_q1, lambda_k1, lambda_q2, lambda_k2, subln_g, ssm_A_re, ssm_A_im, ssm_log_dt, ssm_B_re, ssm_B_im, ssm_C_re, ssm_C_im, ssm_D, w_glu, b_glu, w_out, norm2_g, peer_w_q, peer_keys, peer_u, peer_v):
    assert w_in.shape[0] == 1, "single-layer trunk"
    bp, seq, _ = x_prompt.shape
    db, ds, _ = x_sample.shape
    lp_real = seq + N_META
    assert lp_real % SUBLANES == 0 and ds == SUBLANES
    tm_p, t_attn, tb, ec = 768, 512, 512, 2048
    lp = _round_up(lp_real, math.lcm(tm_p, t_attn, tb))
    past = page_table.shape[1] * cache_k.shape[2]

    gm = jnp.kron(jnp.eye(ATTN_WIDTH // ATTN_DK, dtype=F32),
                  jnp.full((ATTN_DK, ATTN_DK), 1.0 / ATTN_DK, F32)).astype(BF16)
    lamp = jnp.stack([lambda_q1[0], lambda_k1[0], lambda_q2[0], lambda_k2[0]]).astype(F32)
    ssm_c = _ssm_consts(ssm_A_re[0], ssm_A_im[0], ssm_log_dt[0], ssm_B_re[0], ssm_B_im[0],
                        ssm_C_re[0], ssm_C_im[0], ssm_D[0], w_glu[0], b_glu[0])
    weights = (norm1_g[0].astype(F32)[None, :], w_in[0].astype(BF16),
               jnp.tile(q_norm_g[0].astype(F32), ATTN_WIDTH // ATTN_DK)[None, :],
               jnp.tile(k_norm_g[0].astype(F32), ATTN_WIDTH // ATTN_DK)[None, :],
               gm, ssm_c,
               jnp.tile(subln_g[0].astype(F32), N_HEADS)[None, :], w_out[0].astype(BF16),
               norm2_g[0].astype(F32)[None, :], peer_w_q[0].T.astype(BF16),
               peer_keys[0].reshape(2 * PEER_HEADS, N_KEYS, N_KEYS).astype(BF16),
               peer_u[0].astype(BF16), peer_v[0].T.astype(BF16))

    hp = jnp.concatenate([jnp.broadcast_to(meta_tokens.astype(F32)[None], (bp, N_META, D_MODEL)),
                          x_prompt.astype(F32),
                          jnp.zeros((bp, lp - lp_real, D_MODEL), F32)], axis=1)

    def attend_prompt(u, qb, kb, vb, consts):
        ys, st = _ssm(u, tm_p, lp_real - 1, consts)
        return _attn_prompt(lamp, qb, kb, vb, t_attn), ys, st

    hp_out, k_p, v_p, st_p = _layer(hp, lp_real, jnp.arange(lp, dtype=F32), weights, tm_p, tb, ec, attend_prompt)

    n_tok = db * ds
    hs = x_sample.astype(F32).reshape(1, n_tok, D_MODEL)
    h0 = jnp.concatenate([state_ssm_re[0].reshape(db, SSM_FLAT), state_ssm_im[0].reshape(db, SSM_FLAT)],
                         axis=1).astype(F32)[None]
    n_pool, page = cache_k.shape[1], cache_k.shape[2]
    ckt = jnp.transpose(cache_k[0], (0, 2, 3, 4, 1)).reshape(n_pool, ATTN_WIDTH, page)
    cv = cache_v[0].reshape(n_pool, page * N_HEADS, ATTN_DV)

    def attend_sample(u, qb, kb, vb, consts):
        ys, st = _ssm(u, 256, None, consts, h0=h0)
        seqs = lambda a: a.reshape(db, ds, a.shape[-1])
        o = _attn_sample(lamp, seqs(qb), seqs(kb), seqs(vb), ckt, cv, page_table, 16)
        return o.reshape(1, n_tok, ATTN_WIDTH), ys, st

    pos_s = jnp.tile(past + jnp.arange(ds, dtype=F32), db)
    hs_out, k_s, v_s, st_s = _layer(hs, n_tok, pos_s, weights, 512, tb, ec, attend_sample)

    y_prompt = hp_out[:, N_META:lp_real].astype(x_prompt.dtype)
    y_sample = hs_out.reshape(db, ds, D_MODEL).astype(x_sample.dtype)
    row = (lp_real - 1) % SUBLANES
    kd, vd, sd = cache_k.dtype, cache_v.dtype, state_ssm_re.dtype
    return (y_prompt, y_sample,
            k_p.reshape(1, bp, lp_real, N_HEADS, 2, ATTN_DK).astype(kd),
            v_p.reshape(1, bp, lp_real, N_HEADS, ATTN_DV).astype(vd),
            st_p[:, row, :SSM_FLAT].reshape(1, bp, SSM_GROUPS, SSM_STATE).astype(sd),
            st_p[:, row, SSM_FLAT:].reshape(1, bp, SSM_GROUPS, SSM_STATE).astype(state_ssm_im.dtype),
            k_s.reshape(1, db, ds, N_HEADS, 2, ATTN_DK).astype(kd),
            v_s.reshape(1, db, ds, N_HEADS, ATTN_DV).astype(vd),
            st_s[0, ds - 1::ds, :SSM_FLAT].reshape(1, db, SSM_GROUPS, SSM_STATE).astype(sd),
            st_s[0, ds - 1::ds, SSM_FLAT:].reshape(1, db, SSM_GROUPS, SSM_STATE).astype(state_ssm_im.dtype))
```
